```python
import jax, jax.numpy as jnp
from jax import lax
import numpy as np

D_MODEL = 1024
BATCH = 16
SEQ = 2048
DEPTH = 2

PLE_DIM = 256
ATT_HEADS = 8
ATT_KV_HEADS = 2
HEAD_DIM = 64
Q_RANK = 256
IDX_HEADS = 8
IDX_DIM = 64
TOPK_MAX = 256
Q_BLOCK = 128
ML_HEADS = 4
ML_DIM = 128
CONV_W = 4
CHUNK = 64
D_FF = 4 * D_MODEL
ROPE_THETA = 500000.0
ROT_DIM = HEAD_DIM // 4
EPS = 1e-6

ATT_W = ATT_HEADS * HEAD_DIM
KV_W = ATT_KV_HEADS * HEAD_DIM
ML_W = ML_HEADS * ML_DIM
MIX_W = ATT_W + ML_W
IN_SIZES = (Q_RANK, KV_W, KV_W, IDX_DIM, IDX_HEADS, ML_W, ML_W, ML_W, ML_W, ML_HEADS, ML_HEADS)
IN_W = sum(IN_SIZES)
IDX_SCALE = (IDX_HEADS ** -0.5) * (IDX_DIM ** -0.5)

kernel_name = "hybrid_dsa_mlstm_parallel_heads"


def rms_norm(x, g):
    xf = x.astype(jnp.float32)
    y = xf * lax.rsqrt(jnp.mean(xf * xf, axis=-1, keepdims=True) + EPS)
    return (y * g.astype(jnp.float32)).astype(x.dtype)


def rope_tables(positions):
    half = ROT_DIM // 2
    inv = ROPE_THETA ** (-(jnp.arange(half, dtype=jnp.float32) * 2.0) / ROT_DIM)
    ang = positions.astype(jnp.float32)[..., None] * inv
    return jnp.cos(ang), jnp.sin(ang)


def apply_partial_rope(x, cos, sin):
    half = ROT_DIM // 2
    c = cos[:, :, None, :].astype(x.dtype)
    s = sin[:, :, None, :].astype(x.dtype)
    x1 = x[..., :half]
    x2 = x[..., half:ROT_DIM]
    return jnp.concatenate([x1 * c - x2 * s, x2 * c + x1 * s, x[..., ROT_DIM:]], axis=-1)


def causal_dwconv(x, w, b):
    S = x.shape[1]
    xp = jnp.pad(x, ((0, 0), (CONV_W - 1, 0), (0, 0)))
    acc = xp[:, 0:S] * w[0]
    for j in range(1, CONV_W):
        acc = acc + xp[:, j:j + S] * w[j]
    return acc + b


def dsa_attention(q, k, v, q_idx, k_idx, w_idx):
    B, S, H, D = q.shape
    G = H // ATT_KV_HEADS
    n_sel = min(TOPK_MAX, S // 4)
    nb = S // Q_BLOCK
    key_pos = jnp.arange(S)
    scale = D ** -0.5

    def blk(a):
        return a.reshape(B, nb, Q_BLOCK, *a.shape[2:]).swapaxes(0, 1)

    def one_block(args):
        qb, qib, wb, qpos = args
        logits = jnp.einsum('bqhd,bsd->bqhs', qib, k_idx)
        score = jnp.einsum('bqh,bqhs->bqs', wb, jax.nn.relu(logits)).astype(jnp.float32)
        causal = key_pos[None, :] <= qpos[:, None]
        score = jnp.where(causal[None], score, -jnp.inf)
        _, sel = lax.top_k(score, n_sel)
        k_sel = jax.vmap(lambda kb, ib: kb[ib])(k, sel)
        v_sel = jax.vmap(lambda vb, ib: vb[ib])(v, sel)
        valid = sel <= qpos[None, :, None]
        qg = qb.reshape(B, Q_BLOCK, ATT_KV_HEADS, G, D)
        s = jnp.einsum('bqgrd,bqkgd->bqgrk', qg, k_sel).astype(jnp.float32) * scale
        s = jnp.where(valid[:, :, None, None, :], s, -jnp.inf)
        pr = jax.nn.softmax(s, axis=-1).astype(v.dtype)
        o = jnp.einsum('bqgrk,bqkgd->bqgrd', pr, v_sel)
        return o.reshape(B, Q_BLOCK, H, D)

    out = lax.map(one_block, (blk(q), blk(q_idx), blk(w_idx), key_pos.reshape(nb, Q_BLOCK)))
    return out.swapaxes(0, 1).reshape(B, S, H, D)


def mlstm_chunkwise(q, k, v, i_pre, f_pre):
    B, S, H, D = q.shape
    nc = S // CHUNK
    f32 = jnp.float32
    qf = q.astype(f32) * (D ** -0.5)
    kf = k.astype(f32)
    vf = v.astype(f32)
    log_f = jax.nn.log_sigmoid(f_pre.astype(f32))
    log_i = i_pre.astype(f32)

    def chunks(a):
        a = a.reshape(B, nc, CHUNK, H, *a.shape[3:])
        return jnp.moveaxis(a, (1, 3), (0, 2))

    tri = jnp.tril(jnp.ones((CHUNK, CHUNK), dtype=bool))

    def step(carry, xs):
        C, n, m = carry
        qc, kc, vc, lic, lfc = xs
        b = jnp.cumsum(lfc, axis=-1)
        log_d = b[..., :, None] - b[..., None, :] + lic[..., None, :]
        log_d = jnp.where(tri, log_d, -jnp.inf)
        inter = b + m[..., None]
        m_t = jnp.maximum(inter, jnp.max(log_d, axis=-1))
        dmat = jnp.exp(log_d - m_t[..., None])
        inter_w = jnp.exp(inter - m_t)
        qk = jnp.einsum('bhtd,bhsd->bhts', qc, kc) * dmat
        num = inter_w[..., None] * jnp.einsum('bhtd,bhde->bhte', qc, C) + jnp.einsum('bhts,bhse->bhte', qk, vc)
        den = inter_w * jnp.einsum('bhtd,bhd->bht', qc, n) + jnp.sum(qk, axis=-1)
        h = num / jnp.maximum(jnp.abs(den), jnp.exp(-m_t))[..., None]
        b_last = b[..., -1]
        log_g = b_last[..., None] - b + lic
        m_new = jnp.maximum(b_last + m, jnp.max(log_g, axis=-1))
        g = jnp.exp(log_g - m_new[..., None])
        decay = jnp.exp(b_last + m - m_new)
        C_new = decay[..., None, None] * C + jnp.einsum('bhs,bhsd,bhse->bhde', g, kc, vc)
        n_new = decay[..., None] * n + jnp.einsum('bhs,bhsd->bhd', g, kc)
        return (C_new, n_new, m_new), h

    init = (jnp.zeros((B, H, D, D), f32), jnp.zeros((B, H, D), f32), jnp.zeros((B, H), f32))
    _, hs = lax.scan(step, init, (chunks(qf), chunks(kf), chunks(vf), chunks(log_i), chunks(log_f)))
    hs = jnp.moveaxis(hs, (0, 2), (1, 3)).reshape(B, S, H, D)
    return hs.astype(q.dtype)


def hybrid_layer(h, p_i, cos, sin, g_mix, w_in, g_cq, w_q_up, w_iq_up, g_qn, g_kn, g_ik,
                 conv_w, conv_b, i_bias, f_bias, g_mh, w_out, g_mlp, w_ff1, w_ff2,
                 g_ple, w_ple_gate, b_ple_gate, w_ple):
    B, S, _ = h.shape
    xn = rms_norm(h, g_mix)
    proj = xn @ w_in
    split_at = np.cumsum(IN_SIZES)[:-1].tolist()
    c_q, a_k, a_v, i_k, i_w, m_q, m_k, m_v, m_o, m_i, m_f = jnp.split(proj, split_at, axis=-1)

    c_q = rms_norm(c_q, g_cq)
    a_q = (c_q @ w_q_up).reshape(B, S, ATT_HEADS, HEAD_DIM)
    i_q = (c_q @ w_iq_up).reshape(B, S, IDX_HEADS, IDX_DIM)
    a_q = apply_partial_rope(rms_norm(a_q, g_qn), cos, sin)
    a_k = apply_partial_rope(rms_norm(a_k.reshape(B, S, ATT_KV_HEADS, HEAD_DIM), g_kn), cos, sin)
    a_v = a_v.reshape(B, S, ATT_KV_HEADS, HEAD_DIM)
    i_q = apply_partial_rope(i_q, cos, sin)
    i_k = apply_partial_rope(rms_norm(i_k, g_ik)[:, :, None, :], cos, sin)[:, :, 0, :]
    i_w = i_w * IDX_SCALE
    att = dsa_attention(a_q, a_k, a_v, i_q, i_k, i_w).reshape(B, S, ATT_W)

    qk = jax.nn.silu(causal_dwconv(jnp.concatenate([m_q, m_k], axis=-1), conv_w, conv_b))
    m_q, m_k = qk[..., :ML_W], qk[..., ML_W:]
    hm = mlstm_chunkwise(m_q.reshape(B, S, ML_HEADS, ML_DIM), m_k.reshape(B, S, ML_HEADS, ML_DIM),
                         m_v.reshape(B, S, ML_HEADS, ML_DIM), m_i + i_bias, m_f + f_bias)
    hm = (jax.nn.sigmoid(m_o) * rms_norm(hm, g_mh).reshape(B, S, ML_W))

    h = h + jnp.concatenate([att, hm], axis=-1) @ w_out

    u = rms_norm(h, g_mlp) @ w_ff1
    h = h + jnp.square(jax.nn.relu(u)) @ w_ff2

    gate = jax.nn.sigmoid(rms_norm(h, g_ple) @ w_ple_gate + b_ple_gate)
    return h + gate * (p_i @ w_ple)


def setup_inputs(seed: int = 0) -> dict:
    key = jax.random.key(seed)
    ks = iter(jax.random.split(key, 32))
    f32 = jnp.float32

    def nrm(shape, scale):
        return jax.random.normal(next(ks), shape, f32) * scale

    def gain(shape):
        return 1.0 + nrm(shape, 0.02)

    L = DEPTH
    x = nrm((BATCH, SEQ, D_MODEL), 1.0)
    p = nrm((L, BATCH, SEQ, PLE_DIM), 1.0)
    offs = jax.random.randint(next(ks), (BATCH, 1), 0, 4096, dtype=jnp.int32)
    positions = jnp.arange(SEQ, dtype=jnp.int32)[None, :] + offs
    return {
        "x": x,
        "p": p,
        "positions": positions,
        "g_mix": gain((L, D_MODEL)),
        "w_in": nrm((L, D_MODEL, IN_W), D_MODEL ** -0.5),
        "g_cq": gain((L, Q_RANK)),
        "w_q_up": nrm((L, Q_RANK, ATT_W), Q_RANK ** -0.5),
        "w_iq_up": nrm((L, Q_RANK, IDX_HEADS * IDX_DIM), Q_RANK ** -0.5),
        "g_qn": gain((L, HEAD_DIM)),
        "g_kn": gain((L, HEAD_DIM)),
        "g_ik": gain((L, IDX_DIM)),
        "conv_w": nrm((L, CONV_W, 2 * ML_W), CONV_W ** -0.5),
        "conv_b": nrm((L, 2 * ML_W), 0.01),
        "i_bias": nrm((L, ML_HEADS), 0.1),
        "f_bias": jnp.linspace(3.0, 6.0, ML_HEADS, dtype=f32)[None, :] + nrm((L, ML_HEADS), 0.1),
        "g_mh": gain((L, ML_HEADS, ML_DIM)),
        "w_out": nrm((L, MIX_W, D_MODEL), MIX_W ** -0.5),
        "g_mlp": gain((L, D_MODEL)),
        "w_ff1": nrm((L, D_MODEL, D_FF), D_MODEL ** -0.5),
        "w_ff2": nrm((L, D_FF, D_MODEL), D_FF ** -0.5),
        "g_ple": gain((L, D_MODEL)),
        "w_ple_gate": nrm((L, D_MODEL, D_MODEL), D_MODEL ** -0.5),
        "b_ple_gate": nrm((L, D_MODEL), 0.01),
        "w_ple": nrm((L, PLE_DIM, D_MODEL), PLE_DIM ** -0.5),
    }


def reference(x, p, positions, g_mix, w_in, g_cq, w_q_up, w_iq_up, g_qn, g_kn, g_ik,
              conv_w, conv_b, i_bias, f_bias, g_mh, w_out, g_mlp, w_ff1, w_ff2,
              g_ple, w_ple_gate, b_ple_gate, w_ple):
    cos, sin = rope_tables(positions)
    h = x
    for i in range(DEPTH):
        h = hybrid_layer(h, p[i], cos, sin, g_mix[i], w_in[i], g_cq[i], w_q_up[i], w_iq_up[i],
                         g_qn[i], g_kn[i], g_ik[i], conv_w[i], conv_b[i], i_bias[i], f_bias[i],
                         g_mh[i], w_out[i], g_mlp[i], w_ff1[i], w_ff2[i], g_ple[i],
                         w_ple_gate[i], b_ple_gate[i], w_ple[i])
    return h
```

```python
import functools

import numpy as np
import jax
import jax.numpy as jnp
from jax import lax
from jax.experimental import pallas as pl
from jax.experimental.pallas import tpu as pltpu

D_MODEL = 1024
PLE_DIM = 256
ATT_HEADS = 8
ATT_KV_HEADS = 2
HEAD_DIM = 64
Q_RANK = 256
IDX_HEADS = 8
IDX_DIM = 64
TOPK_MAX = 256
ML_HEADS = 4
ML_DIM = 128
CONV_W = 4
D_FF = 4 * D_MODEL
ROPE_THETA = 500000.0
ROT_DIM = HEAD_DIM // 4
ROT_HALF = ROT_DIM // 2
EPS = 1e-6

ATT_W = ATT_HEADS * HEAD_DIM
KV_W = ATT_KV_HEADS * HEAD_DIM
ML_W = ML_HEADS * ML_DIM
IDX_W = IDX_HEADS * IDX_DIM
IN_SIZES = (Q_RANK, KV_W, KV_W, IDX_DIM, IDX_HEADS, ML_W, ML_W, ML_W, ML_W, ML_HEADS, ML_HEADS)
IN_W = sum(IN_SIZES)
IDX_SCALE = (IDX_HEADS ** -0.5) * (IDX_DIM ** -0.5)
ATT_SCALE = HEAD_DIM ** -0.5
GQA = ATT_HEADS // ATT_KV_HEADS

LANES = 128
OFF_CQ, OFF_AK, OFF_AV = 0, Q_RANK, Q_RANK + KV_W
OFF_MQ = OFF_AV + KV_W
OFF_MK, OFF_MV, OFF_MO = OFF_MQ + ML_W, OFF_MQ + 2 * ML_W, OFF_MQ + 3 * ML_W
OFF_MISC = OFF_MQ + 4 * ML_W
IN_W_PAD = OFF_MISC + LANES
MISC_IK, MISC_IW, MISC_MI, MISC_MF = 0, IDX_DIM, IDX_DIM + IDX_HEADS, IDX_DIM + IDX_HEADS + ML_HEADS

PROJ_TM = 512
MIX_TM = 512
FF_CHUNK = 1024
DSA_TQ = 256
ML_CHUNK = 256
NEG_BIG = -1e30
VMEM_LIMIT = 56 * 1024 * 1024

_NT = (((1,), (1,)), ((), ()))

f32 = jnp.float32
bf16 = jnp.bfloat16
i32 = jnp.int32
INT_MIN = -2 ** 31


def _dot(a, b):
    return jnp.dot(a, b, preferred_element_type=f32)


def _dot_nt(a, b):
    return lax.dot_general(a, b, _NT, preferred_element_type=f32)


def _sigmoid(x):
    return 1.0 / (1.0 + jnp.exp(-x))


def _proj_kernel(h_ref, pos_ref, gmix_ref, win_ref, wgt_ref, gcq_ref, wqup_ref, gq_ref, gk_ref, gik_ref,
                 invf_ref, sga_ref, sgb_ref, seg_ref,
                 aq_ref, iq_ref, ak_ref, av_ref, ik_ref, misc_ref, gt_ref, mq_ref, mk_ref, mv_ref, mo_ref):
    x = h_ref[...]
    ms = jnp.mean(x * x, axis=-1, keepdims=True)
    xn = (x * lax.rsqrt(ms + EPS) * gmix_ref[...]).astype(bf16)
    proj = _dot(xn, win_ref[...])
    gt_ref[...] = _dot_nt(wgt_ref[...], xn)

    ang = pos_ref[...].astype(f32) * invf_ref[...]
    cos_t = jnp.cos(ang)
    sin_t = jnp.sin(ang)
    sa_t = sin_t * sga_ref[...]
    sb_t = sin_t * sgb_ref[...]

    def rope(v):
        k = v.shape[1] // LANES
        w = v.shape[1]
        c = jnp.concatenate([cos_t] * k, axis=1) if k > 1 else cos_t
        sa = jnp.concatenate([sa_t] * k, axis=1) if k > 1 else sa_t
        sb = jnp.concatenate([sb_t] * k, axis=1) if k > 1 else sb_t
        return v * c + pltpu.roll(v, w - ROT_HALF, 1) * sa + pltpu.roll(v, ROT_HALF, 1) * sb

    def seg_mean_sq(v):
        w = v.shape[1]
        sq = v * v
        hi = sq.astype(bf16)
        lo = (sq - hi.astype(f32)).astype(bf16)
        seg = seg_ref[0:w, 0:w]
        return (_dot(hi, seg) + _dot(lo, seg)) * (1.0 / HEAD_DIM)

    cq = proj[:, OFF_CQ:OFF_CQ + Q_RANK]
    cqn = (cq * lax.rsqrt(jnp.mean(cq * cq, axis=-1, keepdims=True) + EPS) * gcq_ref[...]).astype(bf16)
    qq = _dot(cqn, wqup_ref[...])
    aq = qq[:, :ATT_W]
    aq = rope(aq * lax.rsqrt(seg_mean_sq(aq) + EPS) * gq_ref[...])
    aq_ref[...] = (aq * ATT_SCALE).astype(bf16)
    iq_ref[...] = rope(qq[:, ATT_W:]).astype(bf16)

    ak = proj[:, OFF_AK:OFF_AK + KV_W]
    ak_ref[...] = rope(ak * lax.rsqrt(seg_mean_sq(ak) + EPS) * gk_ref[...]).astype(bf16)
    av_ref[...] = proj[:, OFF_AV:OFF_AV + KV_W].astype(bf16)

    misc = proj[:, OFF_MISC:OFF_MISC + LANES]
    ikn = rope(misc * lax.rsqrt(seg_mean_sq(misc) + EPS) * gik_ref[...])
    ik_ref[...] = ikn[:, MISC_IK:MISC_IK + IDX_DIM].astype(bf16)
    misc_ref[...] = misc

    mq_ref[...] = proj[:, OFF_MQ:OFF_MQ + ML_W]
    mk_ref[...] = proj[:, OFF_MK:OFF_MK + ML_W]
    mv_ref[...] = proj[:, OFF_MV:OFF_MV + ML_W].astype(bf16)
    mo_ref[...] = proj[:, OFF_MO:OFF_MO + ML_W]


def _proj_call(h, pos, gmix, win, wgt, gcq, wqup, gq, gk, gik, invf, sga, sgb, seg):
    n = h.shape[0]
    tm = min(PROJ_TM, n)
    grid = (n // tm,)

    def tok(w):
        return pl.BlockSpec((tm, w), lambda i: (i, 0))

    def full(a):
        return pl.BlockSpec(a.shape, lambda i: (0,) * a.ndim)

    out_shape = [
        jax.ShapeDtypeStruct((n, ATT_W), bf16),
        jax.ShapeDtypeStruct((n, IDX_W), bf16),
        jax.ShapeDtypeStruct((n, KV_W), bf16),
        jax.ShapeDtypeStruct((n, KV_W), bf16),
        jax.ShapeDtypeStruct((n, IDX_DIM), bf16),
        jax.ShapeDtypeStruct((n, LANES), f32),
        jax.ShapeDtypeStruct((2 * ML_HEADS, n), f32),
        jax.ShapeDtypeStruct((n, ML_W), f32),
        jax.ShapeDtypeStruct((n, ML_W), f32),
        jax.ShapeDtypeStruct((n, ML_W), bf16),
        jax.ShapeDtypeStruct((n, ML_W), f32),
    ]
    out_specs = [tok(ATT_W), tok(IDX_W), tok(KV_W), tok(KV_W), tok(IDX_DIM), tok(LANES),
                 pl.BlockSpec((2 * ML_HEADS, tm), lambda i: (0, i)),
                 tok(ML_W), tok(ML_W), tok(ML_W), tok(ML_W)]
    in_specs = [tok(D_MODEL), tok(1)] + [full(a) for a in (gmix, win, wgt, gcq, wqup, gq, gk, gik, invf, sga, sgb, seg)]
    return pl.pallas_call(
        _proj_kernel, grid=grid, in_specs=in_specs, out_specs=out_specs, out_shape=out_shape,
        compiler_params=pltpu.CompilerParams(dimension_semantics=("parallel",), vmem_limit_bytes=VMEM_LIMIT),
        name="proj",
    )(h, pos, gmix, win, wgt, gcq, wqup, gq, gk, gik, invf, sga, sgb, seg)


def _dsa_kernel(aq_ref, iq_ref, misc_ref, ak_ref, av_ref, ik_ref, o_ref,
                key_ref, bias_ref, cut_ref, m_ref, l_ref, acc_ref, *, n_sel, tq):
    tk = tq
    qi = pl.program_id(1)
    nk = qi + 1
    q_pos = lax.broadcasted_iota(i32, (tq, tk), 0) + qi * tq
    col_l = lax.broadcasted_iota(i32, (tq, tk), 1)
    w_idx = misc_ref[:, MISC_IW:MISC_IW + IDX_HEADS] * IDX_SCALE

    def idx_body(kc, carry):
        off = pl.multiple_of(kc * tk, tk)
        ikc = ik_ref[pl.ds(off, tk), :]
        s = jnp.zeros((tq, tk), f32)
        for hh in range(IDX_HEADS):
            lg = _dot_nt(iq_ref[:, hh * IDX_DIM:(hh + 1) * IDX_DIM], ikc)
            s = s + w_idx[:, hh:hh + 1] * jnp.maximum(lg, 0.0)
        bits = pltpu.bitcast(s, i32)
        key = bits ^ ((bits >> 31) & 0x7FFFFFFF)
        key = jnp.where(bits == INT_MIN, 0, key)
        key = jnp.where(col_l + kc * tk <= q_pos, key, INT_MIN)
        key_ref[kc] = key
        return carry

    lax.fori_loop(0, nk, idx_body, 0)

    def count(pred):
        def body(kc, part):
            m = pred(key_ref[kc], col_l + kc * tk).astype(f32)
            return part + m[:, :LANES] + m[:, LANES:]
        part = lax.fori_loop(0, nk, body, jnp.zeros((tq, LANES), f32))
        return jnp.sum(part, axis=1, keepdims=True)

    def bit_body(it, carry):
        t_u, cnt_t = carry
        cand_u = t_u | lax.shift_left(jnp.int32(1), 31 - it)
        cand_s = cand_u ^ INT_MIN
        cnt = count(lambda k, idx: k >= cand_s)
        ok = cnt >= n_sel
        return jnp.where(ok, cand_u, t_u), jnp.where(ok, cnt, cnt_t)

    n_keys = (nk * tk).astype(f32)
    t_u, cnt_t = lax.fori_loop(0, 32, bit_body,
                               (jnp.zeros((tq, 1), i32), jnp.zeros((tq, 1), f32) + n_keys))
    t_s = t_u ^ INT_MIN

    cut_ref[...] = jnp.full((tq, 1), 2 ** 30, i32)

    @pl.when(jnp.max(cnt_t) > n_sel)
    def _():
        need = n_sel - count(lambda k, idx: k > t_s)
        c = jnp.zeros((tq, 1), i32)
        n_bits = max(1, int(ak_ref.shape[0] - 1).bit_length())
        for b in range(n_bits, -1, -1):
            cand = c | (1 << b)
            g = count(lambda k, idx: (k == t_s) & (idx < cand))
            c = jnp.where(g <= need, cand, c)
        cut_ref[...] = c

    cut = cut_ref[...]

    def bias_body(kc, carry):
        k = key_ref[kc]
        idx = col_l + kc * tk
        sel = ((k > t_s) | ((k == t_s) & (idx < cut))) & (idx <= q_pos)
        bias_ref[kc] = jnp.where(sel, 0.0, NEG_BIG)
        return carry

    lax.fori_loop(0, nk, bias_body, 0)

    m_ref[...] = jnp.full(m_ref.shape, NEG_BIG, f32)
    l_ref[...] = jnp.zeros(l_ref.shape, f32)
    acc_ref[...] = jnp.zeros(acc_ref.shape, f32)

    def att_body(kc, carry):
        off = pl.multiple_of(kc * tk, tk)
        bias = bias_ref[kc]
        for g in range(ATT_KV_HEADS):
            kg = ak_ref[pl.ds(off, tk), g * HEAD_DIM:(g + 1) * HEAD_DIM]
            vg = av_ref[pl.ds(off, tk), g * HEAD_DIM:(g + 1) * HEAD_DIM]
            for r in range(GQA):
                hh = g * GQA + r
                s = _dot_nt(aq_ref[:, hh * HEAD_DIM:(hh + 1) * HEAD_DIM], kg) + bias
                m_old = m_ref[hh]
                m_new = jnp.maximum(m_old, jnp.max(s, axis=1, keepdims=True))
                p = jnp.exp(s - m_new)
                alpha = jnp.exp(m_old - m_new)
                l_ref[hh] = alpha * l_ref[hh] + jnp.sum(p, axis=1, keepdims=True)
                acc_ref[hh] = alpha * acc_ref[hh] + _dot(p.astype(bf16), vg)
                m_ref[hh] = m_new
        return carry

    lax.fori_loop(0, nk, att_body, 0)
    o_ref[...] = jnp.concatenate([acc_ref[hh] / l_ref[hh] for hh in range(ATT_HEADS)], axis=1).astype(o_ref.dtype)


def _dsa_call(aq, iq, misc, ak, av, ik, n_sel):
    b, s, _ = aq.shape
    tq = min(DSA_TQ, s)
    nq = s // tq

    def qblk(w):
        return pl.BlockSpec((None, tq, w), lambda bi, qi: (bi, qi, 0))

    def kblk(w):
        return pl.BlockSpec((None, s, w), lambda bi, qi: (bi, 0, 0))

    return pl.pallas_call(
        functools.partial(_dsa_kernel, n_sel=n_sel, tq=tq),
        grid=(b, nq),
        in_specs=[qblk(ATT_W), qblk(IDX_W), qblk(LANES), kblk(KV_W), kblk(KV_W), kblk(IDX_DIM)],
        out_specs=qblk(ATT_W),
        out_shape=jax.ShapeDtypeStruct((b, s, ATT_W), bf16),
        scratch_shapes=[
            pltpu.VMEM((nq, tq, tq), i32),
            pltpu.VMEM((nq, tq, tq), f32),
            pltpu.VMEM((tq, 1), i32),
            pltpu.VMEM((ATT_HEADS, tq, 1), f32),
            pltpu.VMEM((ATT_HEADS, tq, 1), f32),
            pltpu.VMEM((ATT_HEADS, tq, HEAD_DIM), f32),
        ],
        compiler_params=pltpu.CompilerParams(dimension_semantics=("parallel", "arbitrary"),
                                             vmem_limit_bytes=VMEM_LIMIT),
        name="dsa",
    )(aq, iq, misc, ak, av, ik)


def _mlstm_kernel(mq_ref, mk_ref, mv_ref, mo_ref, gt_ref, gb_ref, cwq_ref, cwk_ref, cbq_ref, cbk_ref, gmh_ref,
                  o_ref, *, chunk):
    hd = pl.program_id(1)
    seq = mq_ref.shape[0]
    n_chunks = seq // chunk
    row = lax.broadcasted_iota(i32, (chunk, chunk), 0)
    col = lax.broadcasted_iota(i32, (chunk, chunk), 1)
    tril = col <= row
    eye = col == row
    row_d = lax.broadcasted_iota(i32, (chunk, ML_DIM), 0)

    def conv_silu(x_ref, w_ref, b_ref, c):
        t0 = c * chunk
        cur = x_ref[t0:t0 + chunk, :]
        acc = cur * w_ref[CONV_W - 1:CONV_W, :] + b_ref[...]
        tail = x_ref[t0 - 8:t0, :] if c > 0 else jnp.zeros((8, ML_DIM), f32)
        tail_ext = jnp.concatenate([jnp.zeros((chunk - 8, ML_DIM), f32), tail], axis=0)
        for j in range(1, CONV_W):
            shifted = jnp.where(row_d >= j, pltpu.roll(cur, j, 0), pltpu.roll(tail_ext, j, 0))
            acc = acc + shifted * w_ref[CONV_W - 1 - j:CONV_W - j, :]
        return acc * _sigmoid(acc)

    c_state = jnp.zeros((ML_DIM, ML_DIM), f32)
    n_state = jnp.zeros((1, ML_DIM), f32)
    m_state = jnp.zeros((1, 1), f32)
    bias_i = gb_ref[pl.ds(hd, 1), :]
    bias_f = gb_ref[pl.ds(ML_HEADS + hd, 1), :]
    for c in range(n_chunks):
        t0 = c * chunk
        q = conv_silu(mq_ref, cwq_ref, cbq_ref, c) * (ML_DIM ** -0.5)
        k = conv_silu(mk_ref, cwk_ref, cbk_ref, c)
        qb = q.astype(bf16)
        kb = k.astype(bf16)
        vb = mv_ref[t0:t0 + chunk, :]
        li_row = gt_ref[pl.ds(hd, 1), t0:t0 + chunk] + bias_i
        f_row = gt_ref[pl.ds(ML_HEADS + hd, 1), t0:t0 + chunk] + bias_f
        lf_row = -(jnp.maximum(-f_row, 0.0) + jnp.log1p(jnp.exp(-jnp.abs(f_row))))
        b_col = jnp.sum(jnp.where(tril, lf_row, 0.0), axis=1, keepdims=True)
        b_row = jnp.sum(jnp.where(eye, b_col, 0.0), axis=0, keepdims=True)
        li_col = jnp.sum(jnp.where(eye, li_row, 0.0), axis=1, keepdims=True)
        b_last = jnp.sum(lf_row, axis=1, keepdims=True)

        log_d = jnp.where(tril, b_col - b_row + li_row, -jnp.inf)
        inter = b_col + m_state
        m_t = jnp.maximum(inter, jnp.max(log_d, axis=1, keepdims=True))
        dmat = jnp.exp(log_d - m_t)
        inter_w = jnp.exp(inter - m_t)
        qk = _dot_nt(qb, kb) * dmat
        num = inter_w * _dot(qb, c_state.astype(bf16)) + _dot(qk.astype(bf16), vb)
        den = inter_w * jnp.sum(q * n_state, axis=1, keepdims=True) + jnp.sum(qk, axis=1, keepdims=True)
        h_t = num / jnp.maximum(jnp.abs(den), jnp.exp(-m_t))

        log_g = b_last - b_col + li_col
        m_new = jnp.maximum(b_last + m_state, jnp.max(log_g, axis=0, keepdims=True))
        g = jnp.exp(log_g - m_new)
        decay = jnp.exp(b_last + m_state - m_new)
        gk = g * k
        c_state = decay * c_state + _dot(gk.T.astype(bf16), vb)
        n_state = decay * n_state + jnp.sum(gk, axis=0, keepdims=True)
        m_state = m_new

        hn = h_t * lax.rsqrt(jnp.mean(h_t * h_t, axis=1, keepdims=True) + EPS) * gmh_ref[...]
        o_ref[t0:t0 + chunk, :] = (_sigmoid(mo_ref[t0:t0 + chunk, :]) * hn).astype(o_ref.dtype)


def _mlstm_call(mq, mk, mv, mo, gt, gb, conv_w, conv_b, gmh):
    b, s, _ = mq.shape
    chunk = min(ML_CHUNK, s)

    def head_blk():
        return pl.BlockSpec((None, s, ML_DIM), lambda bi, hi: (bi, 0, hi))

    return pl.pallas_call(
        functools.partial(_mlstm_kernel, chunk=chunk),
        grid=(b, ML_HEADS),
        in_specs=[head_blk(), head_blk(), head_blk(), head_blk(),
                  pl.BlockSpec((2 * ML_HEADS, s), lambda bi, hi: (0, bi)),
                  pl.BlockSpec((2 * ML_HEADS, 1), lambda bi, hi: (0, 0)),
                  pl.BlockSpec((CONV_W, ML_DIM), lambda bi, hi: (0, hi)),
                  pl.BlockSpec((CONV_W, ML_DIM), lambda bi, hi: (0, ML_HEADS + hi)),
                  pl.BlockSpec((1, ML_DIM), lambda bi, hi: (0, hi)),
                  pl.BlockSpec((1, ML_DIM), lambda bi, hi: (0, ML_HEADS + hi)),
                  pl.BlockSpec((None, 1, ML_DIM), lambda bi, hi: (hi, 0, 0))],
        out_specs=head_blk(),
        out_shape=jax.ShapeDtypeStruct((b, s, ML_W), bf16),
        compiler_params=pltpu.CompilerParams(dimension_semantics=("parallel", "parallel"),
                                             vmem_limit_bytes=VMEM_LIMIT),
        name="mlstm",
    )(mq, mk, mv, mo, gt, gb, conv_w, conv_w, conv_b, conv_b, gmh)


def _mix_kernel(h_ref, att_ref, hm_ref, p_ref, wo_ref, gmlp_ref, w1_ref, w2_ref, gple_ref, wg_ref, bg_ref, wp_ref,
                o_ref):
    def rms(v, g_ref):
        return (v * lax.rsqrt(jnp.mean(v * v, axis=-1, keepdims=True) + EPS) * g_ref[...]).astype(bf16)

    mixed = _dot(att_ref[...], wo_ref[0:ATT_W, :]) + _dot(hm_ref[...], wo_ref[ATT_W:ATT_W + ML_W, :])
    h1 = h_ref[...] + mixed
    xn = rms(h1, gmlp_ref)
    mlp = None
    for f in range(D_FF // FF_CHUNK):
        u = jnp.maximum(_dot(xn, w1_ref[:, f * FF_CHUNK:(f + 1) * FF_CHUNK]), 0.0)
        part = _dot((u * u).astype(bf16), w2_ref[f * FF_CHUNK:(f + 1) * FF_CHUNK, :])
        mlp = part if mlp is None else mlp + part
    h2 = h1 + mlp
    gate = _sigmoid(_dot(rms(h2, gple_ref), wg_ref[...]) + bg_ref[...])
    o_ref[...] = h2 + gate * _dot(p_ref[...].astype(bf16), wp_ref[...])


def _mix_call(h, att, hm, p, wo, gmlp, w1, w2, gple, wg, bg, wp):
    n = h.shape[0]
    tm = min(MIX_TM, n)

    def tok(w):
        return pl.BlockSpec((tm, w), lambda i: (i, 0))

    def full(a):
        return pl.BlockSpec(a.shape, lambda i: (0,) * a.ndim, pipeline_mode=pl.Buffered(1))

    return pl.pallas_call(
        _mix_kernel, grid=(n // tm,),
        in_specs=[tok(D_MODEL), tok(ATT_W), tok(ML_W), tok(PLE_DIM)]
                 + [full(a) for a in (wo, gmlp, w1, w2, gple, wg, bg, wp)],
        out_specs=tok(D_MODEL),
        out_shape=jax.ShapeDtypeStruct((n, D_MODEL), f32),
        compiler_params=pltpu.CompilerParams(dimension_semantics=("parallel",), vmem_limit_bytes=VMEM_LIMIT),
        name="mix",
    )(h, att, hm, p, wo, gmlp, w1, w2, gple, wg, bg, wp)


def _rope_rows():
    lane = np.arange(LANES) % HEAD_DIM
    inv = ROPE_THETA ** (-(jnp.arange(ROT_HALF, dtype=f32) * 2.0) / ROT_DIM)
    invf = jnp.where(lane < ROT_DIM, inv[lane % ROT_HALF], 0.0).astype(f32)[None, :]
    sga = jnp.asarray(np.where(lane < ROT_HALF, -1.0, 0.0), f32)[None, :]
    sgb = jnp.asarray(np.where((lane >= ROT_HALF) & (lane < ROT_DIM), 1.0, 0.0), f32)[None, :]
    return invf, sga, sgb


def kernel(x, p, positions, g_mix, w_in, g_cq, w_q_up, w_iq_up, g_qn, g_kn, g_ik, conv_w, conv_b, i_bias, f_bias,
           g_mh, w_out, g_mlp, w_ff1, w_ff2, g_ple, w_ple_gate, b_ple_gate, w_ple):
    b, s, d = x.shape
    n = b * s
    depth = p.shape[0]
    n_sel = min(TOPK_MAX, s // 4)
    h = x.reshape(n, d)
    pos = positions.reshape(n, 1).astype(i32)
    invf, sga, sgb = _rope_rows()
    seg_id = np.arange(ATT_W) // HEAD_DIM
    seg = jnp.asarray(seg_id[:, None] == seg_id[None, :], bf16)

    split = np.cumsum(IN_SIZES)[:-1].tolist()
    for i in range(depth):
        c_q, a_k, a_v, i_k, i_w, m_q, m_k, m_v, m_o, m_i, m_f = jnp.split(w_in[i], split, axis=1)
        pad = jnp.zeros((d, LANES - IDX_DIM - IDX_HEADS - 2 * ML_HEADS), f32)
        win = jnp.concatenate([c_q, a_k, a_v, m_q, m_k, m_v, m_o, i_k, i_w, m_i, m_f, pad], axis=1).astype(bf16)
        wgt = jnp.concatenate([m_i, m_f], axis=1).T.astype(bf16)
        wqup = jnp.concatenate([w_q_up[i], w_iq_up[i]], axis=1).astype(bf16)
        gq = jnp.tile(g_qn[i], ATT_HEADS)[None, :]
        gk = jnp.tile(g_kn[i], ATT_KV_HEADS)[None, :]
        gik = jnp.concatenate([g_ik[i], jnp.zeros((LANES - IDX_DIM,), f32)])[None, :]
        aq, iq, ak, av, ik, misc, gt, mq, mk, mv, mo = _proj_call(
            h, pos, g_mix[i][None, :], win, wgt, g_cq[i][None, :], wqup, gq, gk, gik, invf, sga, sgb, seg)

        def b3(a):
            return a.reshape(b, s, a.shape[-1])

        att = _dsa_call(b3(aq), b3(iq), b3(misc), b3(ak), b3(av), b3(ik), n_sel)
        gb = jnp.concatenate([i_bias[i], f_bias[i]])[:, None]
        hm = _mlstm_call(b3(mq), b3(mk), b3(mv), b3(mo), gt, gb, conv_w[i], conv_b[i][None, :],
                         g_mh[i][:, None, :])
        h = _mix_call(h, att.reshape(n, ATT_W), hm.reshape(n, ML_W), p[i].reshape(n, PLE_DIM),
                      w_out[i].astype(bf16), g_mlp[i][None, :], w_ff1[i].astype(bf16), w_ff2[i].astype(bf16),
                      g_ple[i][None, :], w_ple_gate[i].astype(bf16), b_ple_gate[i][None, :], w_ple[i].astype(bf16))
    return h.reshape(b, s, d)
```

```python
import functools

import numpy as np
import jax
import jax.numpy as jnp
from jax import lax
from jax.experimental import pallas as pl
from jax.experimental.pallas import tpu as pltpu

D_MODEL = 1024
PLE_DIM = 256
ATT_HEADS = 8
ATT_KV_HEADS = 2
HEAD_DIM = 64
Q_RANK = 256
IDX_HEADS = 8
IDX_DIM = 64
TOPK_MAX = 256
ML_HEADS = 4
ML_DIM = 128
CONV_W = 4
D_FF = 4 * D_MODEL
ROPE_THETA = 500000.0
ROT_DIM = HEAD_DIM // 4
ROT_HALF = ROT_DIM // 2
EPS = 1e-6

ATT_W = ATT_HEADS * HEAD_DIM
KV_W = ATT_KV_HEADS * HEAD_DIM
ML_W = ML_HEADS * ML_DIM
IDX_W = IDX_HEADS * IDX_DIM
IN_SIZES = (Q_RANK, KV_W, KV_W, IDX_DIM, IDX_HEADS, ML_W, ML_W, ML_W, ML_W, ML_HEADS, ML_HEADS)
IDX_SCALE = (IDX_HEADS ** -0.5) * (IDX_DIM ** -0.5)
ATT_SCALE = HEAD_DIM ** -0.5
GQA = ATT_HEADS // ATT_KV_HEADS

LANES = 128
SUBLANES = 8
OFF_CQ = 0
OFF_MQ = Q_RANK
OFF_MK, OFF_MV, OFF_MO = OFF_MQ + ML_W, OFF_MQ + 2 * ML_W, OFF_MQ + 3 * ML_W
MAIN_W = OFF_MQ + 4 * ML_W
ROW_AK, ROW_IK, ROW_AV = 0, KV_W, KV_W + IDX_DIM
ROW_GT = ROW_AV + KV_W
GT_IW, GT_MI, GT_MF = 0, IDX_HEADS, IDX_HEADS + ML_HEADS
GT_ROWS = IDX_HEADS + 2 * ML_HEADS
T_ROWS = ROW_GT + GT_ROWS

PROJ_TM = 512
MIX_TM = 512
FF_CHUNK = 1024
DSA_T = 256
ML_CHUNK = 256
NEG_BIG = -1e30
VMEM_LIMIT = 56 * 1024 * 1024

_NT = (((1,), (1,)), ((), ()))

f32 = jnp.float32
bf16 = jnp.bfloat16
i32 = jnp.int32
INT_MIN = -2 ** 31


def _dot(a, b):
    return jnp.dot(a, b, preferred_element_type=f32)


def _dot_nt(a, b):
    return lax.dot_general(a, b, _NT, preferred_element_type=f32)


def _sigmoid(x):
    return 1.0 / (1.0 + jnp.exp(-x))


def _fold_rows(x, op):
    x = x.reshape(x.shape[0] // SUBLANES, SUBLANES, x.shape[1])
    while x.shape[0] > 1:
        half = x.shape[0] // 2
        x = op(x[:half], x[half:])
    return x[0]


def _proj_kernel(h_ref, pos_ref, gmix_ref, wmain_ref, wt_ref, gcq_ref, wqupt_ref, gq_ref, gk_ref, gik_ref, invf_ref,
                 aqt_ref, iqt_ref, ak_ref, ik_ref, avt_ref, gt_ref, mq_ref, mk_ref, mv_ref, mo_ref):
    x = h_ref[...]
    ms = jnp.mean(x * x, axis=-1, keepdims=True)
    xn = (x * lax.rsqrt(ms + EPS) * gmix_ref[...]).astype(bf16)
    proj = _dot(xn, wmain_ref[...])
    pt = _dot_nt(wt_ref[...], xn)

    ang = invf_ref[...] * pos_ref[...].astype(f32)
    cos_t = jnp.cos(ang)
    sin_t = jnp.sin(ang)

    def rope_t(blk):
        x1 = blk[0:ROT_HALF]
        x2 = blk[ROT_HALF:ROT_DIM]
        return jnp.concatenate([x1 * cos_t - x2 * sin_t, x2 * cos_t + x1 * sin_t, blk[ROT_DIM:]], axis=0)

    def norm_t(blk, g_ref):
        return blk * lax.rsqrt(jnp.mean(blk * blk, axis=0, keepdims=True) + EPS) * g_ref[...]

    def head(a, j):
        return a[j * HEAD_DIM:(j + 1) * HEAD_DIM]

    cq = proj[:, OFF_CQ:OFF_CQ + Q_RANK]
    cqn = (cq * lax.rsqrt(jnp.mean(cq * cq, axis=-1, keepdims=True) + EPS) * gcq_ref[...]).astype(bf16)
    qqt = _dot_nt(wqupt_ref[...], cqn)
    aqt = jnp.concatenate([rope_t(norm_t(head(qqt, j), gq_ref)) for j in range(ATT_HEADS)], axis=0)
    aqt_ref[...] = (aqt * ATT_SCALE).astype(bf16)
    iqt = jnp.concatenate([rope_t(head(qqt, ATT_HEADS + j)) for j in range(IDX_HEADS)], axis=0)
    iqt_ref[...] = iqt.astype(bf16)

    akt = jnp.concatenate([rope_t(norm_t(head(pt, j), gk_ref)) for j in range(ATT_KV_HEADS)], axis=0)
    ak_ref[...] = akt.T.astype(bf16)
    ikt = rope_t(norm_t(pt[ROW_IK:ROW_IK + IDX_DIM], gik_ref))
    ikt = jnp.concatenate([ikt, jnp.zeros((LANES - IDX_DIM, ikt.shape[1]), f32)], axis=0)
    ik_ref[...] = ikt.T.astype(bf16)
    avt = pt[ROW_AV:ROW_AV + KV_W].astype(bf16)
    tk = avt_ref.shape[2]
    for j in range(avt_ref.shape[0]):
        avt_ref[j] = avt[:, j * tk:(j + 1) * tk]
    gt_ref[...] = pt[ROW_GT:ROW_GT + GT_ROWS]

    mq_ref[...] = proj[:, OFF_MQ:OFF_MQ + ML_W]
    mk_ref[...] = proj[:, OFF_MK:OFF_MK + ML_W]
    mv_ref[...] = proj[:, OFF_MV:OFF_MV + ML_W].astype(bf16)
    mo_ref[...] = proj[:, OFF_MO:OFF_MO + ML_W]


def _proj_call(h, pos, gmix, wmain, wt, gcq, wqupt, gq, gk, gik, invf, tk):
    n = h.shape[0]
    tm = min(PROJ_TM, n)
    grid = (n // tm,)

    def tok(w):
        return pl.BlockSpec((tm, w), lambda i: (i, 0))

    def tok_t(r):
        return pl.BlockSpec((r, tm), lambda i: (0, i))

    def full(a):
        return pl.BlockSpec(a.shape, lambda i: (0,) * a.ndim)

    out_shape = [
        jax.ShapeDtypeStruct((ATT_W, n), bf16),
        jax.ShapeDtypeStruct((IDX_W, n), bf16),
        jax.ShapeDtypeStruct((n, KV_W), bf16),
        jax.ShapeDtypeStruct((n, LANES), bf16),
        jax.ShapeDtypeStruct((n // tk, KV_W, tk), bf16),
        jax.ShapeDtypeStruct((GT_ROWS, n), f32),
        jax.ShapeDtypeStruct((n, ML_W), f32),
        jax.ShapeDtypeStruct((n, ML_W), f32),
        jax.ShapeDtypeStruct((n, ML_W), bf16),
        jax.ShapeDtypeStruct((n, ML_W), f32),
    ]
    out_specs = [tok_t(ATT_W), tok_t(IDX_W), tok(KV_W), tok(LANES),
                 pl.BlockSpec((tm // tk, KV_W, tk), lambda i: (i, 0, 0)), tok_t(GT_ROWS),
                 tok(ML_W), tok(ML_W), tok(ML_W), tok(ML_W)]
    in_specs = [tok(D_MODEL), tok_t(1)] + [full(a) for a in (gmix, wmain, wt, gcq, wqupt, gq, gk, gik, invf)]
    return pl.pallas_call(
        _proj_kernel, grid=grid, in_specs=in_specs, out_specs=out_specs, out_shape=out_shape,
        compiler_params=pltpu.CompilerParams(dimension_semantics=("parallel",), vmem_limit_bytes=VMEM_LIMIT),
        name="proj",
    )(h, pos, gmix, wmain, wt, gcq, wqupt, gq, gk, gik, invf)


def _dsa_kernel(aqt_ref, iqt_ref, gt_ref, ak_ref, ik_ref, avt_ref, o_ref,
                key_ref, s_ref, cut_ref, l_ref, acc_ref, *, n_sel, t):
    qi = pl.program_id(1)
    nk = qi + 1
    k_loc = lax.broadcasted_iota(i32, (t, t), 0)
    q_pos = lax.broadcasted_iota(i32, (t, t), 1) + qi * t
    w_idx = gt_ref[GT_IW:GT_IW + IDX_HEADS, :] * IDX_SCALE

    def idx_body(kc, carry):
        off = pl.multiple_of(kc * t, t)
        ikc = ik_ref[pl.ds(off, t), 0:IDX_DIM]
        s = jnp.zeros((t, t), f32)
        for hh in range(IDX_HEADS):
            lg = _dot(ikc, iqt_ref[hh * IDX_DIM:(hh + 1) * IDX_DIM, :])
            s = s + w_idx[hh:hh + 1, :] * jnp.maximum(lg, 0.0)
        bits = pltpu.bitcast(s, i32)
        key = bits ^ ((bits >> 31) & 0x7FFFFFFF)
        key = jnp.where(bits == INT_MIN, 0, key)
        key = jnp.where(k_loc + kc * t <= q_pos, key, INT_MIN)
        key_ref[kc] = key
        return carry

    lax.fori_loop(0, nk, idx_body, 0)

    def count(pred):
        def body(kc, part):
            return part + _fold_rows(pred(key_ref[kc], k_loc + kc * t).astype(f32), jnp.add)
        part = lax.fori_loop(0, nk, body, jnp.zeros((SUBLANES, t), f32))
        return jnp.sum(part, axis=0, keepdims=True)

    def bit_body(it, carry):
        t_u, cnt_t = carry
        cand_u = t_u | lax.shift_left(jnp.int32(1), 31 - it)
        cand_s = cand_u ^ INT_MIN
        cnt = count(lambda k, idx: k >= cand_s)
        ok = cnt >= n_sel
        return jnp.where(ok, cand_u, t_u), jnp.where(ok, cnt, cnt_t)

    n_keys = (nk * t).astype(f32)
    t_u, cnt_t = lax.fori_loop(0, 32, bit_body,
                               (jnp.zeros((1, t), i32), jnp.zeros((1, t), f32) + n_keys))
    t_s = t_u ^ INT_MIN

    cut_ref[...] = jnp.full((1, t), 2 ** 30, i32)

    @pl.when(jnp.max(cnt_t) > n_sel)
    def _():
        need = n_sel - count(lambda k, idx: k > t_s)
        c = jnp.zeros((1, t), i32)
        n_bits = max(1, int(ak_ref.shape[0] - 1).bit_length())
        for b in range(n_bits, -1, -1):
            cand = c | (1 << b)
            g = count(lambda k, idx: (k == t_s) & (idx < cand))
            c = jnp.where(g <= need, cand, c)
        cut_ref[...] = c

    cut = cut_ref[...]

    def logits_body(kc, mx):
        off = pl.multiple_of(kc * t, t)
        k = key_ref[kc]
        idx = k_loc + kc * t
        sel = ((k > t_s) | ((k == t_s) & (idx < cut))) & (idx <= q_pos)
        bias = jnp.where(sel, 0.0, NEG_BIG)
        rows = []
        for g in range(ATT_KV_HEADS):
            kg = ak_ref[pl.ds(off, t), g * HEAD_DIM:(g + 1) * HEAD_DIM]
            for r in range(GQA):
                hh = g * GQA + r
                s = _dot(kg, aqt_ref[hh * HEAD_DIM:(hh + 1) * HEAD_DIM, :]) + bias
                s_ref[hh, kc] = s
                rows.append(jnp.max(_fold_rows(s, jnp.maximum), axis=0, keepdims=True))
        return jnp.maximum(mx, jnp.concatenate(rows, axis=0))

    m_fin = lax.fori_loop(0, nk, logits_body, jnp.full((ATT_HEADS, t), NEG_BIG, f32))

    l_ref[...] = jnp.zeros(l_ref.shape, f32)
    acc_ref[...] = jnp.zeros(acc_ref.shape, f32)

    def pv_body(kc, carry):
        for g in range(ATT_KV_HEADS):
            vtg = avt_ref[kc, g * HEAD_DIM:(g + 1) * HEAD_DIM, :]
            for r in range(GQA):
                hh = g * GQA + r
                p = jnp.exp(s_ref[hh, kc] - m_fin[hh:hh + 1, :])
                l_ref[hh] += _fold_rows(p, jnp.add)
                acc_ref[hh] += _dot(vtg, p.astype(bf16))
        return carry

    lax.fori_loop(0, nk, pv_body, 0)
    att_t = jnp.concatenate(
        [acc_ref[hh] / jnp.sum(l_ref[hh], axis=0, keepdims=True) for hh in range(ATT_HEADS)], axis=0)
    o_ref[...] = att_t.T.astype(o_ref.dtype)


def _dsa_call(aqt, iqt, gt, ak, ik, avt, n_sel, b, s, t):
    nq = s // t

    def qblk_t(r):
        return pl.BlockSpec((r, t), lambda bi, qi: (0, bi * nq + qi))

    def kblk(w):
        return pl.BlockSpec((None, s, w), lambda bi, qi: (bi, 0, 0))

    return pl.pallas_call(
        functools.partial(_dsa_kernel, n_sel=n_sel, t=t),
        grid=(b, nq),
        in_specs=[qblk_t(ATT_W), qblk_t(IDX_W), qblk_t(GT_ROWS), kblk(KV_W), kblk(LANES),
                  pl.BlockSpec((nq, KV_W, t), lambda bi, qi: (bi, 0, 0))],
        out_specs=pl.BlockSpec((None, t, ATT_W), lambda bi, qi: (bi, qi, 0)),
        out_shape=jax.ShapeDtypeStruct((b, s, ATT_W), bf16),
        scratch_shapes=[
            pltpu.VMEM((nq, t, t), i32),
            pltpu.VMEM((ATT_HEADS, nq, t, t), f32),
            pltpu.VMEM((1, t), i32),
            pltpu.VMEM((ATT_HEADS, SUBLANES, t), f32),
            pltpu.VMEM((ATT_HEADS, HEAD_DIM, t), f32),
        ],
        compiler_params=pltpu.CompilerParams(dimension_semantics=("parallel", "arbitrary"),
                                             vmem_limit_bytes=VMEM_LIMIT),
        name="dsa",
    )(aqt, iqt, gt, ak, ik, avt)


def _mlstm_kernel(mq_ref, mk_ref, mv_ref, mo_ref, gt_ref, gb_ref, cwq_ref, cwk_ref, cbq_ref, cbk_ref, gmh_ref,
                  o_ref, *, chunk):
    hd = pl.program_id(1)
    seq = mq_ref.shape[0]
    n_chunks = seq // chunk
    row = lax.broadcasted_iota(i32, (chunk, chunk), 0)
    col = lax.broadcasted_iota(i32, (chunk, chunk), 1)
    tril = col <= row
    eye = col == row
    row_d = lax.broadcasted_iota(i32, (chunk, ML_DIM), 0)

    def conv_silu(x_ref, w_ref, b_ref, c):
        t0 = c * chunk
        cur = x_ref[t0:t0 + chunk, :]
        acc = cur * w_ref[CONV_W - 1:CONV_W, :] + b_ref[...]
        tail = x_ref[t0 - SUBLANES:t0, :] if c > 0 else jnp.zeros((SUBLANES, ML_DIM), f32)
        tail_ext = jnp.concatenate([jnp.zeros((chunk - SUBLANES, ML_DIM), f32), tail], axis=0)
        for j in range(1, CONV_W):
            shifted = jnp.where(row_d >= j, pltpu.roll(cur, j, 0), pltpu.roll(tail_ext, j, 0))
            acc = acc + shifted * w_ref[CONV_W - 1 - j:CONV_W - j, :]
        return acc * _sigmoid(acc)

    c_state = jnp.zeros((ML_DIM, ML_DIM), f32)
    n_state = jnp.zeros((1, ML_DIM), f32)
    m_state = jnp.zeros((1, 1), f32)
    bias_i = gb_ref[pl.ds(hd, 1), :]
    bias_f = gb_ref[pl.ds(ML_HEADS + hd, 1), :]
    for c in range(n_chunks):
        t0 = c * chunk
        q = conv_silu(mq_ref, cwq_ref, cbq_ref, c) * (ML_DIM ** -0.5)
        k = conv_silu(mk_ref, cwk_ref, cbk_ref, c)
        qb = q.astype(bf16)
        kb = k.astype(bf16)
        vb = mv_ref[t0:t0 + chunk, :]
        li_row = gt_ref[pl.ds(GT_MI + hd, 1), t0:t0 + chunk] + bias_i
        f_row = gt_ref[pl.ds(GT_MF + hd, 1), t0:t0 + chunk] + bias_f
        lf_row = -(jnp.maximum(-f_row, 0.0) + jnp.log1p(jnp.exp(-jnp.abs(f_row))))
        b_col = jnp.sum(jnp.where(tril, lf_row, 0.0), axis=1, keepdims=True)
        b_row = jnp.sum(jnp.where(eye, b_col, 0.0), axis=0, keepdims=True)
        li_col = jnp.sum(jnp.where(eye, li_row, 0.0), axis=1, keepdims=True)
        b_last = jnp.sum(lf_row, axis=1, keepdims=True)

        log_d = jnp.where(tril, b_col - b_row + li_row, -jnp.inf)
        inter = b_col + m_state
        m_t = jnp.maximum(inter, jnp.max(log_d, axis=1, keepdims=True))
        dmat = jnp.exp(log_d - m_t)
        inter_w = jnp.exp(inter - m_t)
        qk = _dot_nt(qb, kb) * dmat
        num = inter_w * _dot(qb, c_state.astype(bf16)) + _dot(qk.astype(bf16), vb)
        den = inter_w * jnp.sum(q * n_state, axis=1, keepdims=True) + jnp.sum(qk, axis=1, keepdims=True)
        h_t = num / jnp.maximum(jnp.abs(den), jnp.exp(-m_t))

        log_g = b_last - b_col + li_col
        m_new = jnp.maximum(b_last + m_state, jnp.max(log_g, axis=0, keepdims=True))
        g = jnp.exp(log_g - m_new)
        decay = jnp.exp(b_last + m_state - m_new)
        gk = g * k
        c_state = decay * c_state + _dot(gk.T.astype(bf16), vb)
        n_state = decay * n_state + jnp.sum(gk, axis=0, keepdims=True)
        m_state = m_new

        hn = h_t * lax.rsqrt(jnp.mean(h_t * h_t, axis=1, keepdims=True) + EPS) * gmh_ref[...]
        o_ref[t0:t0 + chunk, :] = (_sigmoid(mo_ref[t0:t0 + chunk, :]) * hn).astype(o_ref.dtype)


def _mlstm_call(mq, mk, mv, mo, gt, gb, conv_w, conv_b, gmh):
    b, s, _ = mq.shape
    chunk = min(ML_CHUNK, s)

    def head_blk():
        return pl.BlockSpec((None, s, ML_DIM), lambda bi, hi: (bi, 0, hi))

    return pl.pallas_call(
        functools.partial(_mlstm_kernel, chunk=chunk),
        grid=(b, ML_HEADS),
        in_specs=[head_blk(), head_blk(), head_blk(), head_blk(),
                  pl.BlockSpec((GT_ROWS, s), lambda bi, hi: (0, bi)),
                  pl.BlockSpec((2 * ML_HEADS, 1), lambda bi, hi: (0, 0)),
                  pl.BlockSpec((CONV_W, ML_DIM), lambda bi, hi: (0, hi)),
                  pl.BlockSpec((CONV_W, ML_DIM), lambda bi, hi: (0, ML_HEADS + hi)),
                  pl.BlockSpec((1, ML_DIM), lambda bi, hi: (0, hi)),
                  pl.BlockSpec((1, ML_DIM), lambda bi, hi: (0, ML_HEADS + hi)),
                  pl.BlockSpec((None, 1, ML_DIM), lambda bi, hi: (hi, 0, 0))],
        out_specs=head_blk(),
        out_shape=jax.ShapeDtypeStruct((b, s, ML_W), bf16),
        compiler_params=pltpu.CompilerParams(dimension_semantics=("parallel", "parallel"),
                                             vmem_limit_bytes=VMEM_LIMIT),
        name="mlstm",
    )(mq, mk, mv, mo, gt, gb, conv_w, conv_w, conv_b, conv_b, gmh)


def _mix_kernel(h_ref, att_ref, hm_ref, p_ref, wo_ref, gmlp_ref, w1_ref, w2_ref, gple_ref, wg_ref, bg_ref, wp_ref,
                o_ref):
    def rms(v, g_ref):
        return (v * lax.rsqrt(jnp.mean(v * v, axis=-1, keepdims=True) + EPS) * g_ref[...]).astype(bf16)

    mixed = _dot(att_ref[...], wo_ref[0:ATT_W, :]) + _dot(hm_ref[...], wo_ref[ATT_W:ATT_W + ML_W, :])
    h1 = h_ref[...] + mixed
    xn = rms(h1, gmlp_ref)
    mlp = None
    for f in range(D_FF // FF_CHUNK):
        u = jnp.maximum(_dot(xn, w1_ref[:, f * FF_CHUNK:(f + 1) * FF_CHUNK]), 0.0)
        part = _dot((u * u).astype(bf16), w2_ref[f * FF_CHUNK:(f + 1) * FF_CHUNK, :])
        mlp = part if mlp is None else mlp + part
    h2 = h1 + mlp
    gate = _sigmoid(_dot(rms(h2, gple_ref), wg_ref[...]) + bg_ref[...])
    o_ref[...] = h2 + gate * _dot(p_ref[...].astype(bf16), wp_ref[...])


def _mix_call(h, att, hm, p, wo, gmlp, w1, w2, gple, wg, bg, wp):
    n = h.shape[0]
    tm = min(MIX_TM, n)

    def tok(w):
        return pl.BlockSpec((tm, w), lambda i: (i, 0))

    def full(a):
        return pl.BlockSpec(a.shape, lambda i: (0,) * a.ndim, pipeline_mode=pl.Buffered(1))

    return pl.pallas_call(
        _mix_kernel, grid=(n // tm,),
        in_specs=[tok(D_MODEL), tok(ATT_W), tok(ML_W), tok(PLE_DIM)]
                 + [full(a) for a in (wo, gmlp, w1, w2, gple, wg, bg, wp)],
        out_specs=tok(D_MODEL),
        out_shape=jax.ShapeDtypeStruct((n, D_MODEL), f32),
        compiler_params=pltpu.CompilerParams(dimension_semantics=("parallel",), vmem_limit_bytes=VMEM_LIMIT),
        name="mix",
    )(h, att, hm, p, wo, gmlp, w1, w2, gple, wg, bg, wp)


def kernel(x, p, positions, g_mix, w_in, g_cq, w_q_up, w_iq_up, g_qn, g_kn, g_ik, conv_w, conv_b, i_bias, f_bias,
           g_mh, w_out, g_mlp, w_ff1, w_ff2, g_ple, w_ple_gate, b_ple_gate, w_ple):
    b, s, d = x.shape
    n = b * s
    depth = p.shape[0]
    n_sel = min(TOPK_MAX, s // 4)
    t = min(DSA_T, s)
    h = x.reshape(n, d)
    pos = positions.reshape(1, n).astype(i32)
    invf = (ROPE_THETA ** (-(jnp.arange(ROT_HALF, dtype=f32) * 2.0) / ROT_DIM))[:, None]

    split = np.cumsum(IN_SIZES)[:-1].tolist()
    for i in range(depth):
        c_q, a_k, a_v, i_k, i_w, m_q, m_k, m_v, m_o, m_i, m_f = jnp.split(w_in[i], split, axis=1)
        wmain = jnp.concatenate([c_q, m_q, m_k, m_v, m_o], axis=1).astype(bf16)
        wt = jnp.concatenate([a_k, i_k, a_v, i_w, m_i, m_f], axis=1).T.astype(bf16)
        wqupt = jnp.concatenate([w_q_up[i], w_iq_up[i]], axis=1).T.astype(bf16)
        aqt, iqt, ak, ik, avt, gt, mq, mk, mv, mo = _proj_call(
            h, pos, g_mix[i][None, :], wmain, wt, g_cq[i][None, :], wqupt,
            g_qn[i][:, None], g_kn[i][:, None], g_ik[i][:, None], invf, t)

        def b3(a):
            return a.reshape(b, s, a.shape[-1])

        att = _dsa_call(aqt, iqt, gt, b3(ak), b3(ik), avt, n_sel, b, s, t)
        gb = jnp.concatenate([i_bias[i], f_bias[i]])[:, None]
        hm = _mlstm_call(b3(mq), b3(mk), b3(mv), b3(mo), gt, gb, conv_w[i], conv_b[i][None, :],
                         g_mh[i][:, None, :])
        h = _mix_call(h, att.reshape(n, ATT_W), hm.reshape(n, ML_W), p[i].reshape(n, PLE_DIM),
                      w_out[i].astype(bf16), g_mlp[i][None, :], w_ff1[i].astype(bf16), w_ff2[i].astype(bf16),
                      g_ple[i][None, :], w_ple_gate[i].astype(bf16), b_ple_gate[i][None, :], w_ple[i].astype(bf16))
    return h.reshape(b, s, d)
```

```python
import functools

import numpy as np
import jax
import jax.numpy as jnp
from jax import lax
from jax.experimental import pallas as pl
from jax.experimental.pallas import tpu as pltpu

D_MODEL = 1024
PLE_DIM = 256
ATT_HEADS = 8
ATT_KV_HEADS = 2
HEAD_DIM = 64
Q_RANK = 256
IDX_HEADS = 8
IDX_DIM = 64
TOPK_MAX = 256
ML_HEADS = 4
ML_DIM = 128
CONV_W = 4
D_FF = 4 * D_MODEL
ROPE_THETA = 500000.0
ROT_DIM = HEAD_DIM // 4
ROT_HALF = ROT_DIM // 2
EPS = 1e-6

ATT_W = ATT_HEADS * HEAD_DIM
KV_W = ATT_KV_HEADS * HEAD_DIM
ML_W = ML_HEADS * ML_DIM
IDX_W = IDX_HEADS * IDX_DIM
IN_SIZES = (Q_RANK, KV_W, KV_W, IDX_DIM, IDX_HEADS, ML_W, ML_W, ML_W, ML_W, ML_HEADS, ML_HEADS)
IDX_SCALE = (IDX_HEADS ** -0.5) * (IDX_DIM ** -0.5)
ATT_SCALE = HEAD_DIM ** -0.5
GQA = ATT_HEADS // ATT_KV_HEADS

LANES = 128
SUBLANES = 8
OFF_CQ = 0
OFF_MQ = Q_RANK
OFF_MK, OFF_MV, OFF_MO = OFF_MQ + ML_W, OFF_MQ + 2 * ML_W, OFF_MQ + 3 * ML_W
MAIN_W = OFF_MQ + 4 * ML_W
ROW_AK, ROW_IK, ROW_AV = 0, KV_W, KV_W + IDX_DIM
ROW_GT = ROW_AV + KV_W
GT_IW, GT_MI, GT_MF = 0, IDX_HEADS, IDX_HEADS + ML_HEADS
GT_PROJ = IDX_HEADS + 2 * ML_HEADS
GT_QN2 = GT_PROJ
GT_KN2 = GT_QN2 + ATT_HEADS
GT_ROWS = 32
T_ROWS = ROW_GT + GT_PROJ
Q_PAD = 2 * HEAD_DIM
V_AUG = HEAD_DIM + 16
KEY_BITS = 32
LOG2E = 1.4426950408889634
BOUND_SLACK = 1.001
BOUND_LIMIT = 40.0

PROJ_TM = 512
MIX_TM = 512
FF_CHUNK = 1024
DSA_T = 256
ML_CHUNK = 256
NEG_BIG = -1e30
VMEM_LIMIT = 56 * 1024 * 1024

_NT = (((1,), (1,)), ((), ()))

f32 = jnp.float32
bf16 = jnp.bfloat16
i32 = jnp.int32
INT_MIN = -2 ** 31


def _dot(a, b):
    return jnp.dot(a, b, preferred_element_type=f32)


def _dot_nt(a, b):
    return lax.dot_general(a, b, _NT, preferred_element_type=f32)


def _sigmoid(x):
    return 1.0 / (1.0 + jnp.exp(-x))


def _bit_transpose32(words):
    a = list(words)
    j, m = 16, 0x0000FFFF
    while j:
        mask = int(np.array(m, np.uint32).view(np.int32))
        k = 0
        while k < 32:
            tmp = (a[k] ^ lax.shift_right_logical(a[k + j], jnp.int32(j))) & mask
            a[k] = a[k] ^ tmp
            a[k + j] = a[k + j] ^ lax.shift_left(tmp, jnp.int32(j))
            k = (k + j + 1) & ~j
        j >>= 1
        m = (m ^ (m << j)) & 0xFFFFFFFF
    return a


def _fold_rows(x, op):
    x = x.reshape(x.shape[0] // SUBLANES, SUBLANES, x.shape[1])
    while x.shape[0] > 1:
        half = x.shape[0] // 2
        x = op(x[:half], x[half:])
    return x[0]


def _proj_kernel(h_ref, pos_ref, gmix_ref, wmain_ref, wt_ref, gcq_ref, wqupt_ref, gq_ref, gk_ref, gik_ref, invf_ref,
                 aqp_ref, iqt_ref, ak_ref, ik_ref, avt_ref, gt_ref, mq_ref, mk_ref, mv_ref, mo_ref):
    x = h_ref[...]
    ms = jnp.mean(x * x, axis=-1, keepdims=True)
    xn = (x * lax.rsqrt(ms + EPS) * gmix_ref[...]).astype(bf16)
    proj = _dot(xn, wmain_ref[...])
    pt = _dot_nt(wt_ref[...], xn)

    ang = invf_ref[...] * pos_ref[...].astype(f32)
    cos_t = jnp.cos(ang)
    sin_t = jnp.sin(ang)

    def rope_t(blk):
        x1 = blk[0:ROT_HALF]
        x2 = blk[ROT_HALF:ROT_DIM]
        return jnp.concatenate([x1 * cos_t - x2 * sin_t, x2 * cos_t + x1 * sin_t, blk[ROT_DIM:]], axis=0)

    def norm_t(blk, g_ref):
        return blk * lax.rsqrt(jnp.mean(blk * blk, axis=0, keepdims=True) + EPS) * g_ref[...]

    def head(a, j):
        return a[j * HEAD_DIM:(j + 1) * HEAD_DIM]

    cq = proj[:, OFF_CQ:OFF_CQ + Q_RANK]
    cqn = (cq * lax.rsqrt(jnp.mean(cq * cq, axis=-1, keepdims=True) + EPS) * gcq_ref[...]).astype(bf16)
    qqt = _dot_nt(wqupt_ref[...], cqn)
    tm = x.shape[0]

    def sq_norm_rows(blk_bf16):
        v = blk_bf16.astype(f32)
        return jnp.sum(v * v, axis=0, keepdims=True)

    zero_h = jnp.zeros((HEAD_DIM, tm), bf16)
    q_blocks, qn2 = [], []
    for j in range(ATT_HEADS):
        qb = (rope_t(norm_t(head(qqt, j), gq_ref)) * (ATT_SCALE * LOG2E)).astype(bf16)
        qn2.append(sq_norm_rows(qb))
        q_blocks += [qb if g == j // GQA else zero_h for g in range(ATT_KV_HEADS)]
    aqp_ref[...] = jnp.concatenate(q_blocks, axis=0)
    iqt = jnp.concatenate([rope_t(head(qqt, ATT_HEADS + j)) for j in range(IDX_HEADS)], axis=0)
    iqt_ref[...] = iqt.astype(bf16)

    akt = jnp.concatenate([rope_t(norm_t(head(pt, j), gk_ref)) for j in range(ATT_KV_HEADS)], axis=0)
    akb = akt.astype(bf16)
    kn2 = [sq_norm_rows(head(akb, g)) for g in range(ATT_KV_HEADS)]
    ak_ref[...] = akt.T.astype(bf16)
    ikt = rope_t(norm_t(pt[ROW_IK:ROW_IK + IDX_DIM], gik_ref))
    ikt = jnp.concatenate([ikt, jnp.zeros((LANES - IDX_DIM, tm), f32)], axis=0)
    ik_ref[...] = ikt.T.astype(bf16)
    avt = pt[ROW_AV:ROW_AV + KV_W].astype(bf16)
    ones = jnp.ones((V_AUG - HEAD_DIM, tm), bf16)
    avaug = jnp.concatenate([blk for g in range(ATT_KV_HEADS) for blk in (head(avt, g), ones)], axis=0)
    tk = avt_ref.shape[2]
    for j in range(avt_ref.shape[0]):
        avt_ref[j] = avaug[:, j * tk:(j + 1) * tk]
    gt_ref[...] = jnp.concatenate(
        [pt[ROW_GT:ROW_GT + GT_PROJ]] + qn2 + kn2
        + [jnp.zeros((GT_ROWS - GT_KN2 - ATT_KV_HEADS, tm), f32)], axis=0)

    mq_ref[...] = proj[:, OFF_MQ:OFF_MQ + ML_W]
    mk_ref[...] = proj[:, OFF_MK:OFF_MK + ML_W]
    mv_ref[...] = proj[:, OFF_MV:OFF_MV + ML_W].astype(bf16)
    mo_ref[...] = proj[:, OFF_MO:OFF_MO + ML_W]


def _proj_call(h, pos, gmix, wmain, wt, gcq, wqupt, gq, gk, gik, invf, tk):
    n = h.shape[0]
    tm = min(PROJ_TM, n)
    grid = (n // tm,)

    def tok(w):
        return pl.BlockSpec((tm, w), lambda i: (i, 0))

    def tok_t(r):
        return pl.BlockSpec((r, tm), lambda i: (0, i))

    def full(a):
        return pl.BlockSpec(a.shape, lambda i: (0,) * a.ndim)

    out_shape = [
        jax.ShapeDtypeStruct((ATT_HEADS * Q_PAD, n), bf16),
        jax.ShapeDtypeStruct((IDX_W, n), bf16),
        jax.ShapeDtypeStruct((n, KV_W), bf16),
        jax.ShapeDtypeStruct((n, LANES), bf16),
        jax.ShapeDtypeStruct((n // tk, ATT_KV_HEADS * V_AUG, tk), bf16),
        jax.ShapeDtypeStruct((GT_ROWS, n), f32),
        jax.ShapeDtypeStruct((n, ML_W), f32),
        jax.ShapeDtypeStruct((n, ML_W), f32),
        jax.ShapeDtypeStruct((n, ML_W), bf16),
        jax.ShapeDtypeStruct((n, ML_W), f32),
    ]
    out_specs = [tok_t(ATT_HEADS * Q_PAD), tok_t(IDX_W), tok(KV_W), tok(LANES),
                 pl.BlockSpec((tm // tk, ATT_KV_HEADS * V_AUG, tk), lambda i: (i, 0, 0)), tok_t(GT_ROWS),
                 tok(ML_W), tok(ML_W), tok(ML_W), tok(ML_W)]
    in_specs = [tok(D_MODEL), tok_t(1)] + [full(a) for a in (gmix, wmain, wt, gcq, wqupt, gq, gk, gik, invf)]
    return pl.pallas_call(
        _proj_kernel, grid=grid, in_specs=in_specs, out_specs=out_specs, out_shape=out_shape,
        compiler_params=pltpu.CompilerParams(dimension_semantics=("parallel",), vmem_limit_bytes=VMEM_LIMIT),
        name="proj",
    )(h, pos, gmix, wmain, wt, gcq, wqupt, gq, gk, gik, invf)


def _dsa_kernel(aqp_ref, iqt_ref, gt_ref, gtk_ref, ak_ref, ik_ref, avt_ref, o_ref,
                key_ref, plane_ref, s_ref, cut_ref, m_ref, acc_ref, *, n_sel, t):
    qi = pl.program_id(1)
    nk = qi + 1
    k_loc = lax.broadcasted_iota(i32, (t, t), 0)
    q_pos = lax.broadcasted_iota(i32, (t, t), 1) + qi * t
    w_idx = gt_ref[GT_IW:GT_IW + IDX_HEADS, :] * IDX_SCALE

    @pl.when(qi == 0)
    def _():
        plane_ref[...] = jnp.zeros(plane_ref.shape, i32)

    def idx_body(kc, carry):
        off = pl.multiple_of(kc * t, t)
        ikc = ik_ref[pl.ds(off, t), 0:IDX_DIM]
        s = jnp.zeros((t, t), f32)
        for hh in range(IDX_HEADS):
            lg = _dot(ikc, iqt_ref[hh * IDX_DIM:(hh + 1) * IDX_DIM, :])
            s = s + w_idx[hh:hh + 1, :] * jnp.maximum(lg, 0.0)
        bits = pltpu.bitcast(s, i32)
        key = bits ^ ((bits >> 31) & 0x7FFFFFFF)
        key = jnp.where(bits == INT_MIN, 0, key)
        key = jnp.where(k_loc + kc * t <= q_pos, key, INT_MIN)
        key_ref[kc] = key
        u = key ^ INT_MIN
        planes = _bit_transpose32([u[SUBLANES * i:SUBLANES * (i + 1), :] for i in range(KEY_BITS)])
        for b in range(KEY_BITS):
            plane_ref[b, kc] = planes[b]
        rows = ak_ref[pl.ds(off, t), :]
        for hh in range(ATT_HEADS):
            s_ref[hh, kc] = _dot(rows, aqp_ref[hh * Q_PAD:(hh + 1) * Q_PAD, :])
        return carry

    lax.fori_loop(0, nk, idx_body, 0)

    def count(pred):
        def body(kc, part):
            return part + _fold_rows(pred(key_ref[kc], k_loc + kc * t).astype(f32), jnp.add)
        part = lax.fori_loop(0, nk, body, jnp.zeros((SUBLANES, t), f32))
        return jnp.sum(part, axis=0, keepdims=True)

    def popcount_rows(words):
        pc = lax.population_count(words)
        return jnp.sum(jnp.sum(pc, axis=0).astype(f32), axis=0, keepdims=True)

    def bit_body(it, carry):
        t_u, above, eq = carry
        ones = eq & plane_ref[it]
        c1 = popcount_rows(ones)
        ok = above + c1 >= n_sel
        t_u = jnp.where(ok, t_u | lax.shift_left(jnp.int32(1), KEY_BITS - 1 - it), t_u)
        return t_u, jnp.where(ok, above, above + c1), jnp.where(ok, ones, eq ^ ones)

    n_chunks = plane_ref.shape[1]
    eq0 = jnp.where(lax.broadcasted_iota(i32, (n_chunks, SUBLANES, t), 0) < nk, -1, 0)
    t_u, above, eq = lax.fori_loop(0, KEY_BITS, bit_body,
                                   (jnp.zeros((1, t), i32), jnp.zeros((1, t), f32), eq0))
    cnt_t = above + popcount_rows(eq)
    t_s = t_u ^ INT_MIN

    cut_ref[...] = jnp.full((1, t), 2 ** 30, i32)

    @pl.when(jnp.max(cnt_t) > n_sel)
    def _():
        need = n_sel - count(lambda k, idx: k > t_s)
        c = jnp.zeros((1, t), i32)
        n_bits = max(1, int(ak_ref.shape[0] - 1).bit_length())
        for b in range(n_bits, -1, -1):
            cand = c | (1 << b)
            g = count(lambda k, idx: (k == t_s) & (idx < cand))
            c = jnp.where(g <= need, cand, c)
        cut_ref[...] = c

    cut = cut_ref[...]

    def sel_bias(kc):
        k = key_ref[kc]
        idx = k_loc + kc * t
        sel = ((k > t_s) | ((k == t_s) & (idx < cut))) & (idx <= q_pos)
        return jnp.where(sel, 0.0, NEG_BIG)

    k_max2 = jnp.max(gtk_ref[GT_KN2:GT_KN2 + ATT_KV_HEADS, :], axis=1, keepdims=True)
    bound = jnp.concatenate(
        [jnp.sqrt(gt_ref[GT_QN2 + hh:GT_QN2 + hh + 1, :] * k_max2[hh // GQA:hh // GQA + 1, :])
         for hh in range(ATT_HEADS)], axis=0) * BOUND_SLACK + (BOUND_SLACK - 1.0)
    m_ref[...] = bound

    @pl.when(jnp.max(bound) > BOUND_LIMIT)
    def _():
        def max_body(kc, mx):
            bias = sel_bias(kc)
            rows = [jnp.max(_fold_rows(s_ref[hh, kc] + bias, jnp.maximum), axis=0, keepdims=True)
                    for hh in range(ATT_HEADS)]
            return jnp.maximum(mx, jnp.concatenate(rows, axis=0))
        m_ref[...] = lax.fori_loop(0, nk, max_body, jnp.full((ATT_HEADS, t), NEG_BIG, f32))

    acc_ref[...] = jnp.zeros(acc_ref.shape, f32)

    def att_body(kc, carry):
        bias = sel_bias(kc)
        for hh in range(ATT_HEADS):
            g = hh // GQA
            p = jnp.exp2(s_ref[hh, kc] + bias - m_ref[hh:hh + 1, :])
            acc_ref[hh] += _dot(avt_ref[kc, g * V_AUG:(g + 1) * V_AUG, :], p.astype(bf16))
        return carry

    lax.fori_loop(0, nk, att_body, 0)
    att_t = jnp.concatenate(
        [acc_ref[hh, 0:HEAD_DIM, :] / acc_ref[hh, HEAD_DIM:HEAD_DIM + 1, :] for hh in range(ATT_HEADS)], axis=0)
    o_ref[...] = att_t.T.astype(o_ref.dtype)


def _dsa_call(aqp, iqt, gt, ak, ik, avt, n_sel, b, s, t):
    nq = s // t

    def qblk_t(r):
        return pl.BlockSpec((r, t), lambda bi, qi: (0, bi * nq + qi))

    def kblk(w):
        return pl.BlockSpec((None, s, w), lambda bi, qi: (bi, 0, 0))

    return pl.pallas_call(
        functools.partial(_dsa_kernel, n_sel=n_sel, t=t),
        grid=(b, nq),
        in_specs=[qblk_t(ATT_HEADS * Q_PAD), qblk_t(IDX_W), qblk_t(GT_ROWS),
                  pl.BlockSpec((GT_ROWS, s), lambda bi, qi: (0, bi)), kblk(KV_W), kblk(LANES),
                  pl.BlockSpec((nq, ATT_KV_HEADS * V_AUG, t), lambda bi, qi: (bi, 0, 0))],
        out_specs=pl.BlockSpec((None, t, ATT_W), lambda bi, qi: (bi, qi, 0)),
        out_shape=jax.ShapeDtypeStruct((b, s, ATT_W), bf16),
        scratch_shapes=[
            pltpu.VMEM((nq, t, t), i32),
            pltpu.VMEM((KEY_BITS, nq, SUBLANES, t), i32),
            pltpu.VMEM((ATT_HEADS, nq, t, t), f32),
            pltpu.VMEM((1, t), i32),
            pltpu.VMEM((ATT_HEADS, t), f32),
            pltpu.VMEM((ATT_HEADS, V_AUG, t), f32),
        ],
        compiler_params=pltpu.CompilerParams(dimension_semantics=("parallel", "arbitrary"),
                                             vmem_limit_bytes=VMEM_LIMIT),
        name="dsa",
    )(aqp, iqt, gt, gt, ak, ik, avt)


def _mlstm_kernel(mq_ref, mk_ref, mv_ref, mo_ref, gt_ref, gb_ref, cwq_ref, cwk_ref, cbq_ref, cbk_ref, gmh_ref,
                  o_ref, *, chunk):
    hd = pl.program_id(1)
    seq = mq_ref.shape[0]
    n_chunks = seq // chunk
    row = lax.broadcasted_iota(i32, (chunk, chunk), 0)
    col = lax.broadcasted_iota(i32, (chunk, chunk), 1)
    tril = col <= row
    eye = col == row
    row_d = lax.broadcasted_iota(i32, (chunk, ML_DIM), 0)

    def conv_silu(x_ref, w_ref, b_ref, c):
        t0 = c * chunk
        cur = x_ref[t0:t0 + chunk, :]
        acc = cur * w_ref[CONV_W - 1:CONV_W, :] + b_ref[...]
        tail = x_ref[t0 - SUBLANES:t0, :] if c > 0 else jnp.zeros((SUBLANES, ML_DIM), f32)
        tail_ext = jnp.concatenate([jnp.zeros((chunk - SUBLANES, ML_DIM), f32), tail], axis=0)
        for j in range(1, CONV_W):
            shifted = jnp.where(row_d >= j, pltpu.roll(cur, j, 0), pltpu.roll(tail_ext, j, 0))
            acc = acc + shifted * w_ref[CONV_W - 1 - j:CONV_W - j, :]
        return acc * _sigmoid(acc)

    c_state = jnp.zeros((ML_DIM, ML_DIM), f32)
    n_state = jnp.zeros((1, ML_DIM), f32)
    m_state = jnp.zeros((1, 1), f32)
    bias_i = gb_ref[pl.ds(hd, 1), :]
    bias_f = gb_ref[pl.ds(ML_HEADS + hd, 1), :]
    for c in range(n_chunks):
        t0 = c * chunk
        q = conv_silu(mq_ref, cwq_ref, cbq_ref, c) * (ML_DIM ** -0.5)
        k = conv_silu(mk_ref, cwk_ref, cbk_ref, c)
        qb = q.astype(bf16)
        kb = k.astype(bf16)
        vb = mv_ref[t0:t0 + chunk, :]
        li_row = gt_ref[pl.ds(GT_MI + hd, 1), t0:t0 + chunk] + bias_i
        f_row = gt_ref[pl.ds(GT_MF + hd, 1), t0:t0 + chunk] + bias_f
        lf_row = -(jnp.maximum(-f_row, 0.0) + jnp.log1p(jnp.exp(-jnp.abs(f_row))))
        b_col = jnp.sum(jnp.where(tril, lf_row, 0.0), axis=1, keepdims=True)
        b_row = jnp.sum(jnp.where(eye, b_col, 0.0), axis=0, keepdims=True)
        li_col = jnp.sum(jnp.where(eye, li_row, 0.0), axis=1, keepdims=True)
        b_last = jnp.sum(lf_row, axis=1, keepdims=True)

        log_d = jnp.where(tril, b_col - b_row + li_row, -jnp.inf)
        inter = b_col + m_state
        m_t = jnp.maximum(inter, jnp.max(log_d, axis=1, keepdims=True))
        dmat = jnp.exp(log_d - m_t)
        inter_w = jnp.exp(inter - m_t)
        qk = _dot_nt(qb, kb) * dmat
        num = inter_w * _dot(qb, c_state.astype(bf16)) + _dot(qk.astype(bf16), vb)
        den = inter_w * jnp.sum(q * n_state, axis=1, keepdims=True) + jnp.sum(qk, axis=1, keepdims=True)
        h_t = num / jnp.maximum(jnp.abs(den), jnp.exp(-m_t))

        log_g = b_last - b_col + li_col
        m_new = jnp.maximum(b_last + m_state, jnp.max(log_g, axis=0, keepdims=True))
        g = jnp.exp(log_g - m_new)
        decay = jnp.exp(b_last + m_state - m_new)
        gk = g * k
        c_state = decay * c_state + _dot(gk.T.astype(bf16), vb)
        n_state = decay * n_state + jnp.sum(gk, axis=0, keepdims=True)
        m_state = m_new

        hn = h_t * lax.rsqrt(jnp.mean(h_t * h_t, axis=1, keepdims=True) + EPS) * gmh_ref[...]
        o_ref[t0:t0 + chunk, :] = (_sigmoid(mo_ref[t0:t0 + chunk, :]) * hn).astype(o_ref.dtype)


def _mlstm_call(mq, mk, mv, mo, gt, gb, conv_w, conv_b, gmh):
    b, s, _ = mq.shape
    chunk = min(ML_CHUNK, s)

    def head_blk():
        return pl.BlockSpec((None, s, ML_DIM), lambda bi, hi: (bi, 0, hi))

    return pl.pallas_call(
        functools.partial(_mlstm_kernel, chunk=chunk),
        grid=(b, ML_HEADS),
        in_specs=[head_blk(), head_blk(), head_blk(), head_blk(),
                  pl.BlockSpec((GT_ROWS, s), lambda bi, hi: (0, bi)),
                  pl.BlockSpec((2 * ML_HEADS, 1), lambda bi, hi: (0, 0)),
                  pl.BlockSpec((CONV_W, ML_DIM), lambda bi, hi: (0, hi)),
                  pl.BlockSpec((CONV_W, ML_DIM), lambda bi, hi: (0, ML_HEADS + hi)),
                  pl.BlockSpec((1, ML_DIM), lambda bi, hi: (0, hi)),
                  pl.BlockSpec((1, ML_DIM), lambda bi, hi: (0, ML_HEADS + hi)),
                  pl.BlockSpec((None, 1, ML_DIM), lambda bi, hi: (hi, 0, 0))],
        out_specs=head_blk(),
        out_shape=jax.ShapeDtypeStruct((b, s, ML_W), bf16),
        compiler_params=pltpu.CompilerParams(dimension_semantics=("parallel", "parallel"),
                                             vmem_limit_bytes=VMEM_LIMIT),
        name="mlstm",
    )(mq, mk, mv, mo, gt, gb, conv_w, conv_w, conv_b, conv_b, gmh)


def _mix_kernel(h_ref, att_ref, hm_ref, p_ref, wo_ref, gmlp_ref, w1_ref, w2_ref, gple_ref, wg_ref, bg_ref, wp_ref,
                o_ref):
    def rms(v, g_ref):
        return (v * lax.rsqrt(jnp.mean(v * v, axis=-1, keepdims=True) + EPS) * g_ref[...]).astype(bf16)

    mixed = _dot(att_ref[...], wo_ref[0:ATT_W, :]) + _dot(hm_ref[...], wo_ref[ATT_W:ATT_W + ML_W, :])
    h1 = h_ref[...] + mixed
    xn = rms(h1, gmlp_ref)
    mlp = None
    for f in range(D_FF // FF_CHUNK):
        u = jnp.maximum(_dot(xn, w1_ref[:, f * FF_CHUNK:(f + 1) * FF_CHUNK]), 0.0)
        part = _dot((u * u).astype(bf16), w2_ref[f * FF_CHUNK:(f + 1) * FF_CHUNK, :])
        mlp = part if mlp is None else mlp + part
    h2 = h1 + mlp
    gate = _sigmoid(_dot(rms(h2, gple_ref), wg_ref[...]) + bg_ref[...])
    o_ref[...] = h2 + gate * _dot(p_ref[...].astype(bf16), wp_ref[...])


def _mix_call(h, att, hm, p, wo, gmlp, w1, w2, gple, wg, bg, wp):
    n = h.shape[0]
    tm = min(MIX_TM, n)

    def tok(w):
        return pl.BlockSpec((tm, w), lambda i: (i, 0))

    def full(a):
        return pl.BlockSpec(a.shape, lambda i: (0,) * a.ndim, pipeline_mode=pl.Buffered(1))

    return pl.pallas_call(
        _mix_kernel, grid=(n // tm,),
        in_specs=[tok(D_MODEL), tok(ATT_W), tok(ML_W), tok(PLE_DIM)]
                 + [full(a) for a in (wo, gmlp, w1, w2, gple, wg, bg, wp)],
        out_specs=tok(D_MODEL),
        out_shape=jax.ShapeDtypeStruct((n, D_MODEL), f32),
        compiler_params=pltpu.CompilerParams(dimension_semantics=("parallel",), vmem_limit_bytes=VMEM_LIMIT),
        name="mix",
    )(h, att, hm, p, wo, gmlp, w1, w2, gple, wg, bg, wp)


def kernel(x, p, positions, g_mix, w_in, g_cq, w_q_up, w_iq_up, g_qn, g_kn, g_ik, conv_w, conv_b, i_bias, f_bias,
           g_mh, w_out, g_mlp, w_ff1, w_ff2, g_ple, w_ple_gate, b_ple_gate, w_ple):
    b, s, d = x.shape
    n = b * s
    depth = p.shape[0]
    n_sel = min(TOPK_MAX, s // 4)
    t = min(DSA_T, s)
    h = x.reshape(n, d)
    pos = positions.reshape(1, n).astype(i32)
    invf = (ROPE_THETA ** (-(jnp.arange(ROT_HALF, dtype=f32) * 2.0) / ROT_DIM))[:, None]

    split = np.cumsum(IN_SIZES)[:-1].tolist()
    for i in range(depth):
        c_q, a_k, a_v, i_k, i_w, m_q, m_k, m_v, m_o, m_i, m_f = jnp.split(w_in[i], split, axis=1)
        wmain = jnp.concatenate([c_q, m_q, m_k, m_v, m_o], axis=1).astype(bf16)
        wt = jnp.concatenate([a_k, i_k, a_v, i_w, m_i, m_f], axis=1).T.astype(bf16)
        wqupt = jnp.concatenate([w_q_up[i], w_iq_up[i]], axis=1).T.astype(bf16)
        aqt, iqt, ak, ik, avt, gt, mq, mk, mv, mo = _proj_call(
            h, pos, g_mix[i][None, :], wmain, wt, g_cq[i][None, :], wqupt,
            g_qn[i][:, None], g_kn[i][:, None], g_ik[i][:, None], invf, t)

        def b3(a):
            return a.reshape(b, s, a.shape[-1])

        att = _dsa_call(aqt, iqt, gt, b3(ak), b3(ik), avt, n_sel, b, s, t)
        gb = jnp.concatenate([i_bias[i], f_bias[i]])[:, None]
        hm = _mlstm_call(b3(mq), b3(mk), b3(mv), b3(mo), gt, gb, conv_w[i], conv_b[i][None, :],
                         g_mh[i][:, None, :])
        h = _mix_call(h, att.reshape(n, ATT_W), hm.reshape(n, ML_W), p[i].reshape(n, PLE_DIM),
                      w_out[i].astype(bf16), g_mlp[i][None, :], w_ff1[i].astype(bf16), w_ff2[i].astype(bf16),
                      g_ple[i][None, :], w_ple_gate[i].astype(bf16), b_ple_gate[i][None, :], w_ple[i].astype(bf16))
    return h.reshape(b, s, d)
```

```python
import functools

import numpy as np
import jax
import jax.numpy as jnp
from jax import lax
from jax.experimental import pallas as pl
from jax.experimental.pallas import tpu as pltpu

D_MODEL = 1024
PLE_DIM = 256
ATT_HEADS = 8
ATT_KV_HEADS = 2
HEAD_DIM = 64
Q_RANK = 256
IDX_HEADS = 8
IDX_DIM = 64
TOPK_MAX = 256
ML_HEADS = 4
ML_DIM = 128
CONV_W = 4
D_FF = 4 * D_MODEL
ROPE_THETA = 500000.0
ROT_DIM = HEAD_DIM // 4
ROT_HALF = ROT_DIM // 2
EPS = 1e-6

ATT_W = ATT_HEADS * HEAD_DIM
KV_W = ATT_KV_HEADS * HEAD_DIM
ML_W = ML_HEADS * ML_DIM
IDX_W = IDX_HEADS * IDX_DIM
IN_SIZES = (Q_RANK, KV_W, KV_W, IDX_DIM, IDX_HEADS, ML_W, ML_W, ML_W, ML_W, ML_HEADS, ML_HEADS)
IDX_SCALE = (IDX_HEADS ** -0.5) * (IDX_DIM ** -0.5)
ATT_SCALE = HEAD_DIM ** -0.5
GQA = ATT_HEADS // ATT_KV_HEADS

LANES = 128
SUBLANES = 8
OFF_CQ = 0
OFF_MQ = Q_RANK
OFF_MK, OFF_MV, OFF_MO = OFF_MQ + ML_W, OFF_MQ + 2 * ML_W, OFF_MQ + 3 * ML_W
MAIN_W = OFF_MQ + 4 * ML_W
ROW_AK, ROW_IK, ROW_AV = 0, KV_W, KV_W + IDX_DIM
ROW_GT = ROW_AV + KV_W
GT_IW, GT_MI, GT_MF = 0, IDX_HEADS, IDX_HEADS + ML_HEADS
GT_PROJ = IDX_HEADS + 2 * ML_HEADS
GT_QN2 = GT_PROJ
GT_KN2 = GT_QN2 + ATT_HEADS
GT_ROWS = 32
T_ROWS = ROW_GT + GT_PROJ
Q_PAD = 2 * HEAD_DIM
V_AUG = HEAD_DIM + 16
KEY_BITS = 32
LOG2E = 1.4426950408889634
BOUND_SLACK = 1.001
BOUND_LIMIT = 40.0

PROJ_TM = 512
MIX_TM = 512
FF_CHUNK = 1024
DSA_T = 256
ML_CHUNK = 256
NEG_BIG = -1e30
VMEM_LIMIT = 56 * 1024 * 1024

_NT = (((1,), (1,)), ((), ()))

f32 = jnp.float32
bf16 = jnp.bfloat16
i32 = jnp.int32
INT_MIN = -2 ** 31


def _dot(a, b):
    return jnp.dot(a, b, preferred_element_type=f32)


def _dot_nt(a, b):
    return lax.dot_general(a, b, _NT, preferred_element_type=f32)


def _sigmoid(x):
    return 0.5 * jnp.tanh(0.5 * x) + 0.5


def _bit_transpose32(words):
    a = list(words)
    j, m = 16, 0x0000FFFF
    while j:
        mask = int(np.array(m, np.uint32).view(np.int32))
        k = 0
        while k < 32:
            tmp = (a[k] ^ lax.shift_right_logical(a[k + j], jnp.int32(j))) & mask
            a[k] = a[k] ^ tmp
            a[k + j] = a[k + j] ^ lax.shift_left(tmp, jnp.int32(j))
            k = (k + j + 1) & ~j
        j >>= 1
        m = (m ^ (m << j)) & 0xFFFFFFFF
    return a


def _fold_rows(x, op):
    x = x.reshape(x.shape[0] // SUBLANES, SUBLANES, x.shape[1])
    while x.shape[0] > 1:
        half = x.shape[0] // 2
        x = op(x[:half], x[half:])
    return x[0]


def _proj_kernel(h_ref, pos_ref, gmix_ref, wmain_ref, wt_ref, gcq_ref, wqupt_ref, gq_ref, gk_ref, gik_ref, invf_ref,
                 aqp_ref, iqt_ref, ak_ref, ik_ref, avt_ref, gt_ref, mq_ref, mk_ref, mv_ref, mo_ref):
    x = h_ref[...]
    ms = jnp.mean(x * x, axis=-1, keepdims=True)
    xn = (x * lax.rsqrt(ms + EPS) * gmix_ref[...]).astype(bf16)
    proj = _dot(xn, wmain_ref[...])
    pt = _dot_nt(wt_ref[...], xn)

    ang = invf_ref[...] * pos_ref[...].astype(f32)
    cos_t = jnp.cos(ang)
    sin_t = jnp.sin(ang)

    def rope_t(blk):
        x1 = blk[0:ROT_HALF]
        x2 = blk[ROT_HALF:ROT_DIM]
        return jnp.concatenate([x1 * cos_t - x2 * sin_t, x2 * cos_t + x1 * sin_t, blk[ROT_DIM:]], axis=0)

    def norm_t(blk, g_ref):
        return blk * lax.rsqrt(jnp.mean(blk * blk, axis=0, keepdims=True) + EPS) * g_ref[...]

    def head(a, j):
        return a[j * HEAD_DIM:(j + 1) * HEAD_DIM]

    cq = proj[:, OFF_CQ:OFF_CQ + Q_RANK]
    cqn = (cq * lax.rsqrt(jnp.mean(cq * cq, axis=-1, keepdims=True) + EPS) * gcq_ref[...]).astype(bf16)
    qqt = _dot_nt(wqupt_ref[...], cqn)
    tm = x.shape[0]

    def sq_norm_rows(blk_bf16):
        v = blk_bf16.astype(f32)
        return jnp.sum(v * v, axis=0, keepdims=True)

    zero_h = jnp.zeros((HEAD_DIM, tm), bf16)
    q_blocks, qn2 = [], []
    for j in range(ATT_HEADS):
        qb = (rope_t(norm_t(head(qqt, j), gq_ref)) * (ATT_SCALE * LOG2E)).astype(bf16)
        qn2.append(sq_norm_rows(qb))
        q_blocks += [qb if g == j // GQA else zero_h for g in range(ATT_KV_HEADS)]
    aqp_ref[...] = jnp.concatenate(q_blocks, axis=0)
    iqt = jnp.concatenate([rope_t(head(qqt, ATT_HEADS + j)) for j in range(IDX_HEADS)], axis=0)
    iqt_ref[...] = iqt.astype(bf16)

    akt = jnp.concatenate([rope_t(norm_t(head(pt, j), gk_ref)) for j in range(ATT_KV_HEADS)], axis=0)
    akb = akt.astype(bf16)
    kn2 = [sq_norm_rows(head(akb, g)) for g in range(ATT_KV_HEADS)]
    ak_ref[...] = akt.T.astype(bf16)
    ikt = rope_t(norm_t(pt[ROW_IK:ROW_IK + IDX_DIM], gik_ref))
    ikt = jnp.concatenate([ikt, jnp.zeros((LANES - IDX_DIM, tm), f32)], axis=0)
    ik_ref[...] = ikt.T.astype(bf16)
    avt = pt[ROW_AV:ROW_AV + KV_W].astype(bf16)
    ones = jnp.ones((V_AUG - HEAD_DIM, tm), bf16)
    avaug = jnp.concatenate([blk for g in range(ATT_KV_HEADS) for blk in (head(avt, g), ones)], axis=0)
    tk = avt_ref.shape[2]
    for j in range(avt_ref.shape[0]):
        avt_ref[j] = avaug[:, j * tk:(j + 1) * tk]
    gt_ref[...] = jnp.concatenate(
        [pt[ROW_GT:ROW_GT + GT_PROJ]] + qn2 + kn2
        + [jnp.zeros((GT_ROWS - GT_KN2 - ATT_KV_HEADS, tm), f32)], axis=0)

    mq_ref[...] = proj[:, OFF_MQ:OFF_MQ + ML_W]
    mk_ref[...] = proj[:, OFF_MK:OFF_MK + ML_W]
    mv_ref[...] = proj[:, OFF_MV:OFF_MV + ML_W].astype(bf16)
    mo_ref[...] = proj[:, OFF_MO:OFF_MO + ML_W]


def _proj_call(h, pos, gmix, wmain, wt, gcq, wqupt, gq, gk, gik, invf, tk):
    n = h.shape[0]
    tm = min(PROJ_TM, n)
    grid = (n // tm,)

    def tok(w):
        return pl.BlockSpec((tm, w), lambda i: (i, 0))

    def tok_t(r):
        return pl.BlockSpec((r, tm), lambda i: (0, i))

    def full(a):
        return pl.BlockSpec(a.shape, lambda i: (0,) * a.ndim)

    out_shape = [
        jax.ShapeDtypeStruct((ATT_HEADS * Q_PAD, n), bf16),
        jax.ShapeDtypeStruct((IDX_W, n), bf16),
        jax.ShapeDtypeStruct((n, KV_W), bf16),
        jax.ShapeDtypeStruct((n, LANES), bf16),
        jax.ShapeDtypeStruct((n // tk, ATT_KV_HEADS * V_AUG, tk), bf16),
        jax.ShapeDtypeStruct((GT_ROWS, n), f32),
        jax.ShapeDtypeStruct((n, ML_W), f32),
        jax.ShapeDtypeStruct((n, ML_W), f32),
        jax.ShapeDtypeStruct((n, ML_W), bf16),
        jax.ShapeDtypeStruct((n, ML_W), f32),
    ]
    out_specs = [tok_t(ATT_HEADS * Q_PAD), tok_t(IDX_W), tok(KV_W), tok(LANES),
                 pl.BlockSpec((tm // tk, ATT_KV_HEADS * V_AUG, tk), lambda i: (i, 0, 0)), tok_t(GT_ROWS),
                 tok(ML_W), tok(ML_W), tok(ML_W), tok(ML_W)]
    in_specs = [tok(D_MODEL), tok_t(1)] + [full(a) for a in (gmix, wmain, wt, gcq, wqupt, gq, gk, gik, invf)]
    return pl.pallas_call(
        _proj_kernel, grid=grid, in_specs=in_specs, out_specs=out_specs, out_shape=out_shape,
        compiler_params=pltpu.CompilerParams(dimension_semantics=("parallel",), vmem_limit_bytes=VMEM_LIMIT),
        name="proj",
    )(h, pos, gmix, wmain, wt, gcq, wqupt, gq, gk, gik, invf)


def _dsa_kernel(aqp_ref, iqt_ref, gt_ref, gtk_ref, ak_ref, ik_ref, avt_ref, o_ref,
                key_ref, plane_ref, s_ref, cut_ref, m_ref, acc_ref, *, n_sel, t):
    qi = pl.program_id(1)
    nk = qi + 1
    k_loc = lax.broadcasted_iota(i32, (t, t), 0)
    q_pos = lax.broadcasted_iota(i32, (t, t), 1) + qi * t
    w_idx = gt_ref[GT_IW:GT_IW + IDX_HEADS, :] * IDX_SCALE

    k_max2 = jnp.max(gtk_ref[GT_KN2:GT_KN2 + ATT_KV_HEADS, :], axis=1, keepdims=True)
    bound = jnp.concatenate(
        [jnp.sqrt(gt_ref[GT_QN2 + hh:GT_QN2 + hh + 1, :] * k_max2[hh // GQA:hh // GQA + 1, :])
         for hh in range(ATT_HEADS)], axis=0) * BOUND_SLACK + (BOUND_SLACK - 1.0)

    @pl.when(qi == 0)
    def _():
        plane_ref[...] = jnp.zeros(plane_ref.shape, i32)

    def idx_body(kc, carry):
        off = pl.multiple_of(kc * t, t)
        ikc = ik_ref[pl.ds(off, t), 0:IDX_DIM]
        s = jnp.zeros((t, t), f32)
        for hh in range(IDX_HEADS):
            lg = _dot(ikc, iqt_ref[hh * IDX_DIM:(hh + 1) * IDX_DIM, :])
            s = s + w_idx[hh:hh + 1, :] * jnp.maximum(lg, 0.0)
        bits = pltpu.bitcast(s, i32)
        key = bits ^ ((bits >> 31) & 0x7FFFFFFF)
        key = jnp.where(bits == INT_MIN, 0, key)
        key = jnp.where(k_loc + kc * t <= q_pos, key, INT_MIN)
        key_ref[kc] = key
        u = key ^ INT_MIN
        planes = _bit_transpose32([u[SUBLANES * i:SUBLANES * (i + 1), :] for i in range(KEY_BITS)])
        for b in range(KEY_BITS):
            plane_ref[b, kc] = planes[b]
        rows = ak_ref[pl.ds(off, t), :]
        for hh in range(ATT_HEADS):
            s_ref[hh, kc] = _dot(rows, aqp_ref[hh * Q_PAD:(hh + 1) * Q_PAD, :])
        return carry

    lax.fori_loop(0, nk, idx_body, 0)

    def popcount_rows(words):
        pc = lax.population_count(words)
        return jnp.sum(jnp.sum(pc, axis=0).astype(f32), axis=0, keepdims=True)

    def bit_body(it, carry):
        t_u, above, eq = carry
        ones = eq & plane_ref[it]
        c1 = popcount_rows(ones)
        ok = above + c1 >= n_sel
        t_u = jnp.where(ok, t_u | lax.shift_left(jnp.int32(1), KEY_BITS - 1 - it), t_u)
        return t_u, jnp.where(ok, above, above + c1), jnp.where(ok, ones, eq ^ ones)

    n_chunks = plane_ref.shape[1]
    word_chunk = lax.broadcasted_iota(i32, (n_chunks, SUBLANES, t), 0)
    eq0 = jnp.where(word_chunk < nk, -1, 0)
    t_u, above, eq = lax.fori_loop(0, KEY_BITS, bit_body,
                                   (jnp.zeros((1, t), i32), jnp.zeros((1, t), f32), eq0))
    cnt_t = above + popcount_rows(eq)
    t_s = t_u ^ INT_MIN

    cut_ref[...] = jnp.full((1, t), 2 ** 30, i32)

    @pl.when(jnp.max(cnt_t) > n_sel)
    def _():
        need = n_sel - above
        first_pos = word_chunk * t + lax.broadcasted_iota(i32, (n_chunks, SUBLANES, t), 1)

        def ties_below(bound_pos):
            n_top = jnp.clip((bound_pos - first_pos + (SUBLANES - 1)) >> 3, 0, KEY_BITS)
            top = lax.shift_right_arithmetic(jnp.full(n_top.shape, INT_MIN, i32), jnp.maximum(n_top - 1, 0))
            return popcount_rows(eq & jnp.where(n_top > 0, top, 0))

        c = jnp.zeros((1, t), i32)
        n_bits = max(1, int(ak_ref.shape[0] - 1).bit_length())
        for b in range(n_bits, -1, -1):
            cand = c | (1 << b)
            c = jnp.where(ties_below(cand) <= need, cand, c)
        cut_ref[...] = c

    cut = cut_ref[...]

    def selected(kc):
        k = key_ref[kc]
        idx = k_loc + kc * t
        return ((k > t_s) | ((k == t_s) & (idx < cut))) & (idx <= q_pos)

    def add_value_products(kc, hh, p_bf16):
        g = hh // GQA
        acc_ref[hh] += _dot(avt_ref[kc, g * V_AUG:(g + 1) * V_AUG, :], p_bf16)

    acc_ref[...] = jnp.zeros(acc_ref.shape, f32)
    shift_is_safe = jnp.max(bound) <= BOUND_LIMIT

    @pl.when(shift_is_safe)
    def _():
        def att_body(kc, carry):
            keep = jnp.where(selected(kc), 1.0, 0.0).astype(bf16)
            for hh in range(ATT_HEADS):
                p = jnp.exp2(s_ref[hh, kc] - bound[hh:hh + 1, :])
                add_value_products(kc, hh, p.astype(bf16) * keep)
            return carry
        lax.fori_loop(0, nk, att_body, 0)

    @pl.when(jnp.logical_not(shift_is_safe))
    def _():
        def max_body(kc, mx):
            bias = jnp.where(selected(kc), 0.0, NEG_BIG)
            rows = [jnp.max(_fold_rows(s_ref[hh, kc] + bias, jnp.maximum), axis=0, keepdims=True)
                    for hh in range(ATT_HEADS)]
            return jnp.maximum(mx, jnp.concatenate(rows, axis=0))
        m_ref[...] = lax.fori_loop(0, nk, max_body, jnp.full((ATT_HEADS, t), NEG_BIG, f32))

        def att_body(kc, carry):
            bias = jnp.where(selected(kc), 0.0, NEG_BIG)
            for hh in range(ATT_HEADS):
                p = jnp.exp2(s_ref[hh, kc] + bias - m_ref[hh:hh + 1, :])
                add_value_products(kc, hh, p.astype(bf16))
            return carry
        lax.fori_loop(0, nk, att_body, 0)

    att_t = jnp.concatenate(
        [acc_ref[hh, 0:HEAD_DIM, :] / acc_ref[hh, HEAD_DIM:HEAD_DIM + 1, :] for hh in range(ATT_HEADS)], axis=0)
    o_ref[...] = att_t.T.astype(o_ref.dtype)


def _dsa_call(aqp, iqt, gt, ak, ik, avt, n_sel, b, s, t):
    nq = s // t

    def qblk_t(r):
        return pl.BlockSpec((r, t), lambda bi, qi: (0, bi * nq + qi))

    def kblk(w):
        return pl.BlockSpec((None, s, w), lambda bi, qi: (bi, 0, 0))

    return pl.pallas_call(
        functools.partial(_dsa_kernel, n_sel=n_sel, t=t),
        grid=(b, nq),
        in_specs=[qblk_t(ATT_HEADS * Q_PAD), qblk_t(IDX_W), qblk_t(GT_ROWS),
                  pl.BlockSpec((GT_ROWS, s), lambda bi, qi: (0, bi)), kblk(KV_W), kblk(LANES),
                  pl.BlockSpec((nq, ATT_KV_HEADS * V_AUG, t), lambda bi, qi: (bi, 0, 0))],
        out_specs=pl.BlockSpec((None, t, ATT_W), lambda bi, qi: (bi, qi, 0)),
        out_shape=jax.ShapeDtypeStruct((b, s, ATT_W), bf16),
        scratch_shapes=[
            pltpu.VMEM((nq, t, t), i32),
            pltpu.VMEM((KEY_BITS, nq, SUBLANES, t), i32),
            pltpu.VMEM((ATT_HEADS, nq, t, t), f32),
            pltpu.VMEM((1, t), i32),
            pltpu.VMEM((ATT_HEADS, t), f32),
            pltpu.VMEM((ATT_HEADS, V_AUG, t), f32),
        ],
        compiler_params=pltpu.CompilerParams(dimension_semantics=("parallel", "arbitrary"),
                                             vmem_limit_bytes=VMEM_LIMIT),
        name="dsa",
    )(aqp, iqt, gt, gt, ak, ik, avt)


def _mlstm_kernel(mq_ref, mk_ref, mv_ref, mo_ref, gt_ref, gb_ref, cwq_ref, cwk_ref, cbq_ref, cbk_ref, gmh_ref,
                  o_ref, *, chunk):
    hd = pl.program_id(1)
    seq = mq_ref.shape[0]
    n_chunks = seq // chunk
    row = lax.broadcasted_iota(i32, (chunk, chunk), 0)
    col = lax.broadcasted_iota(i32, (chunk, chunk), 1)
    tril = col <= row
    eye = col == row
    row_t = lax.broadcasted_iota(i32, (SUBLANES, ML_DIM), 0)

    def conv_silu(x_ref, w_ref, b_ref, c):
        t0 = c * chunk
        cur = x_ref[t0:t0 + chunk, :]
        acc = cur * w_ref[CONV_W - 1:CONV_W, :] + b_ref[...]
        for j in range(1, CONV_W):
            if c > 0:
                shifted = x_ref[t0 - j:t0 - j + chunk, :]
            else:
                rolled = pltpu.roll(cur, j, 0)
                first = jnp.where(row_t >= j, rolled[0:SUBLANES], 0.0)
                shifted = jnp.concatenate([first, rolled[SUBLANES:]], axis=0)
            acc = acc + shifted * w_ref[CONV_W - 1 - j:CONV_W - j, :]
        return acc * _sigmoid(acc)

    c_state = jnp.zeros((ML_DIM, ML_DIM), f32)
    n_state = jnp.zeros((1, ML_DIM), f32)
    m_state = jnp.zeros((1, 1), f32)
    li_all = gt_ref[pl.ds(GT_MI + hd, 1), :] + gb_ref[pl.ds(hd, 1), :]
    f_all = gt_ref[pl.ds(GT_MF + hd, 1), :] + gb_ref[pl.ds(ML_HEADS + hd, 1), :]
    lf_all = -(jnp.maximum(-f_all, 0.0) + jnp.log1p(jnp.exp(-jnp.abs(f_all))))
    for c in range(n_chunks):
        t0 = c * chunk
        q = conv_silu(mq_ref, cwq_ref, cbq_ref, c) * (ML_DIM ** -0.5)
        k = conv_silu(mk_ref, cwk_ref, cbk_ref, c)
        qb = q.astype(bf16)
        kb = k.astype(bf16)
        vb = mv_ref[t0:t0 + chunk, :]
        li_row = li_all[:, t0:t0 + chunk]
        lf_row = lf_all[:, t0:t0 + chunk]
        b_col = jnp.sum(jnp.where(tril, lf_row, 0.0), axis=1, keepdims=True)
        b_row = jnp.sum(jnp.where(eye, b_col, 0.0), axis=0, keepdims=True)
        li_col = jnp.sum(jnp.where(eye, li_row, 0.0), axis=1, keepdims=True)
        b_last = jnp.sum(lf_row, axis=1, keepdims=True)

        log_d = jnp.where(tril, b_col - b_row + li_row, -jnp.inf)
        inter = b_col + m_state
        m_t = jnp.maximum(inter, jnp.max(log_d, axis=1, keepdims=True))
        dmat = jnp.exp(log_d - m_t)
        inter_w = jnp.exp(inter - m_t)
        qk = _dot_nt(qb, kb) * dmat
        num = inter_w * _dot(qb, c_state.astype(bf16)) + _dot(qk.astype(bf16), vb)
        den = inter_w * jnp.sum(q * n_state, axis=1, keepdims=True) + jnp.sum(qk, axis=1, keepdims=True)
        h_t = num / jnp.maximum(jnp.abs(den), jnp.exp(-m_t))

        log_g = b_last - b_col + li_col
        m_new = jnp.maximum(b_last + m_state, jnp.max(log_g, axis=0, keepdims=True))
        g = jnp.exp(log_g - m_new)
        decay = jnp.exp(b_last + m_state - m_new)
        gk = g * k
        c_state = decay * c_state + _dot(gk.T.astype(bf16), vb)
        n_state = decay * n_state + jnp.sum(gk, axis=0, keepdims=True)
        m_state = m_new

        hn = h_t * lax.rsqrt(jnp.mean(h_t * h_t, axis=1, keepdims=True) + EPS) * gmh_ref[...]
        o_ref[t0:t0 + chunk, :] = (_sigmoid(mo_ref[t0:t0 + chunk, :]) * hn).astype(o_ref.dtype)


def _mlstm_call(mq, mk, mv, mo, gt, gb, conv_w, conv_b, gmh):
    b, s, _ = mq.shape
    chunk = min(ML_CHUNK, s)

    def head_blk():
        return pl.BlockSpec((None, s, ML_DIM), lambda bi, hi: (bi, 0, hi))

    return pl.pallas_call(
        functools.partial(_mlstm_kernel, chunk=chunk),
        grid=(b, ML_HEADS),
        in_specs=[head_blk(), head_blk(), head_blk(), head_blk(),
                  pl.BlockSpec((GT_ROWS, s), lambda bi, hi: (0, bi)),
                  pl.BlockSpec((2 * ML_HEADS, 1), lambda bi, hi: (0, 0)),
                  pl.BlockSpec((CONV_W, ML_DIM), lambda bi, hi: (0, hi)),
                  pl.BlockSpec((CONV_W, ML_DIM), lambda bi, hi: (0, ML_HEADS + hi)),
                  pl.BlockSpec((1, ML_DIM), lambda bi, hi: (0, hi)),
                  pl.BlockSpec((1, ML_DIM), lambda bi, hi: (0, ML_HEADS + hi)),
                  pl.BlockSpec((None, 1, ML_DIM), lambda bi, hi: (hi, 0, 0))],
        out_specs=head_blk(),
        out_shape=jax.ShapeDtypeStruct((b, s, ML_W), bf16),
        compiler_params=pltpu.CompilerParams(dimension_semantics=("parallel", "parallel"),
                                             vmem_limit_bytes=VMEM_LIMIT),
        name="mlstm",
    )(mq, mk, mv, mo, gt, gb, conv_w, conv_w, conv_b, conv_b, gmh)


def _mix_kernel(h_ref, att_ref, hm_ref, p_ref, wo_ref, gmlp_ref, w1_ref, w2_ref, gple_ref, wg_ref, bg_ref, wp_ref,
                o_ref):
    def rms(v, g_ref):
        return (v * lax.rsqrt(jnp.mean(v * v, axis=-1, keepdims=True) + EPS) * g_ref[...]).astype(bf16)

    mixed = _dot(att_ref[...], wo_ref[0:ATT_W, :]) + _dot(hm_ref[...], wo_ref[ATT_W:ATT_W + ML_W, :])
    h1 = h_ref[...] + mixed
    xn = rms(h1, gmlp_ref)
    mlp = None
    for f in range(D_FF // FF_CHUNK):
        u = jnp.maximum(_dot(xn, w1_ref[:, f * FF_CHUNK:(f + 1) * FF_CHUNK]), 0.0)
        part = _dot((u * u).astype(bf16), w2_ref[f * FF_CHUNK:(f + 1) * FF_CHUNK, :])
        mlp = part if mlp is None else mlp + part
    h2 = h1 + mlp
    gate = _sigmoid(_dot(rms(h2, gple_ref), wg_ref[...]) + bg_ref[...])
    o_ref[...] = h2 + gate * _dot(p_ref[...].astype(bf16), wp_ref[...])


def _mix_call(h, att, hm, p, wo, gmlp, w1, w2, gple, wg, bg, wp):
    n = h.shape[0]
    tm = min(MIX_TM, n)

    def tok(w):
        return pl.BlockSpec((tm, w), lambda i: (i, 0))

    def full(a):
        return pl.BlockSpec(a.shape, lambda i: (0,) * a.ndim, pipeline_mode=pl.Buffered(1))

    return pl.pallas_call(
        _mix_kernel, grid=(n // tm,),
        in_specs=[tok(D_MODEL), tok(ATT_W), tok(ML_W), tok(PLE_DIM)]
                 + [full(a) for a in (wo, gmlp, w1, w2, gple, wg, bg, wp)],
        out_specs=tok(D_MODEL),
        out_shape=jax.ShapeDtypeStruct((n, D_MODEL), f32),
        compiler_params=pltpu.CompilerParams(dimension_semantics=("parallel",), vmem_limit_bytes=VMEM_LIMIT),
        name="mix",
    )(h, att, hm, p, wo, gmlp, w1, w2, gple, wg, bg, wp)


def kernel(x, p, positions, g_mix, w_in, g_cq, w_q_up, w_iq_up, g_qn, g_kn, g_ik, conv_w, conv_b, i_bias, f_bias,
           g_mh, w_out, g_mlp, w_ff1, w_ff2, g_ple, w_ple_gate, b_ple_gate, w_ple):
    b, s, d = x.shape
    n = b * s
    depth = p.shape[0]
    n_sel = min(TOPK_MAX, s // 4)
    t = min(DSA_T, s)
    h = x.reshape(n, d)
    pos = positions.reshape(1, n).astype(i32)
    invf = (ROPE_THETA ** (-(jnp.arange(ROT_HALF, dtype=f32) * 2.0) / ROT_DIM))[:, None]

    split = np.cumsum(IN_SIZES)[:-1].tolist()
    for i in range(depth):
        c_q, a_k, a_v, i_k, i_w, m_q, m_k, m_v, m_o, m_i, m_f = jnp.split(w_in[i], split, axis=1)
        wmain = jnp.concatenate([c_q, m_q, m_k, m_v, m_o], axis=1).astype(bf16)
        wt = jnp.concatenate([a_k, i_k, a_v, i_w, m_i, m_f], axis=1).T.astype(bf16)
        wqupt = jnp.concatenate([w_q_up[i], w_iq_up[i]], axis=1).T.astype(bf16)
        aqt, iqt, ak, ik, avt, gt, mq, mk, mv, mo = _proj_call(
            h, pos, g_mix[i][None, :], wmain, wt, g_cq[i][None, :], wqupt,
            g_qn[i][:, None], g_kn[i][:, None], g_ik[i][:, None], invf, t)

        def b3(a):
            return a.reshape(b, s, a.shape[-1])

        att = _dsa_call(aqt, iqt, gt, b3(ak), b3(ik), avt, n_sel, b, s, t)
        gb = jnp.concatenate([i_bias[i], f_bias[i]])[:, None]
        hm = _mlstm_call(b3(mq), b3(mk), b3(mv), b3(mo), gt, gb, conv_w[i], conv_b[i][None, :],
                         g_mh[i][:, None, :])
        h = _mix_call(h, att.reshape(n, ATT_W), hm.reshape(n, ML_W), p[i].reshape(n, PLE_DIM),
                      w_out[i].astype(bf16), g_mlp[i][None, :], w_ff1[i].astype(bf16), w_ff2[i].astype(bf16),
                      g_ple[i][None, :], w_ple_gate[i].astype(bf16), b_ple_gate[i][None, :], w_ple[i].astype(bf16))
    return h.reshape(b, s, d)
```

```python
import functools

import numpy as np
import jax
import jax.numpy as jnp
from jax import lax
from jax.experimental import pallas as pl
from jax.experimental.pallas import tpu as pltpu

D_MODEL = 1024
PLE_DIM = 256
ATT_HEADS = 8
ATT_KV_HEADS = 2
HEAD_DIM = 64
Q_RANK = 256
IDX_HEADS = 8
IDX_DIM = 64
TOPK_MAX = 256
ML_HEADS = 4
ML_DIM = 128
CONV_W = 4
D_FF = 4 * D_MODEL
ROPE_THETA = 500000.0
ROT_DIM = HEAD_DIM // 4
ROT_HALF = ROT_DIM // 2
EPS = 1e-6

ATT_W = ATT_HEADS * HEAD_DIM
KV_W = ATT_KV_HEADS * HEAD_DIM
ML_W = ML_HEADS * ML_DIM
IDX_W = IDX_HEADS * IDX_DIM
IN_SIZES = (Q_RANK, KV_W, KV_W, IDX_DIM, IDX_HEADS, ML_W, ML_W, ML_W, ML_W, ML_HEADS, ML_HEADS)
IDX_SCALE = (IDX_HEADS ** -0.5) * (IDX_DIM ** -0.5)
ATT_SCALE = HEAD_DIM ** -0.5
GQA = ATT_HEADS // ATT_KV_HEADS

LANES = 128
SUBLANES = 8
OFF_CQ = 0
OFF_MQ = Q_RANK
OFF_MK, OFF_MV, OFF_MO = OFF_MQ + ML_W, OFF_MQ + 2 * ML_W, OFF_MQ + 3 * ML_W
MAIN_W = OFF_MQ + 4 * ML_W
ROW_AK, ROW_IK, ROW_AV = 0, KV_W, KV_W + IDX_DIM
ROW_GT = ROW_AV + KV_W
GT_IW, GT_MI, GT_MF = 0, IDX_HEADS, IDX_HEADS + ML_HEADS
GT_PROJ = IDX_HEADS + 2 * ML_HEADS
GT_QN2 = GT_PROJ
GT_KN2 = GT_QN2 + ATT_HEADS
GT_ROWS = 32
T_ROWS = ROW_GT + GT_PROJ
Q_PAD = 2 * HEAD_DIM
V_AUG = HEAD_DIM + 16
KEY_BITS = 32
LOG2E = 1.4426950408889634
BOUND_SLACK = 1.001
BOUND_LIMIT = 40.0

PROJ_TM = 512
MIX_TM = 512
FF_CHUNK = 1024
DSA_T = 256
ML_CHUNK = 256
NEG_BIG = -1e30
VMEM_LIMIT = 56 * 1024 * 1024

_NT = (((1,), (1,)), ((), ()))

f32 = jnp.float32
bf16 = jnp.bfloat16
i32 = jnp.int32
INT_MIN = -2 ** 31


def _dot(a, b):
    return jnp.dot(a, b, preferred_element_type=f32)


def _dot_nt(a, b):
    return lax.dot_general(a, b, _NT, preferred_element_type=f32)


def _sigmoid(x):
    return 0.5 * jnp.tanh(0.5 * x) + 0.5


def _bit_transpose32(words):
    a = list(words)
    j, m = 16, 0x0000FFFF
    while j:
        mask = int(np.array(m, np.uint32).view(np.int32))
        k = 0
        while k < 32:
            tmp = (a[k] ^ lax.shift_right_logical(a[k + j], jnp.int32(j))) & mask
            a[k] = a[k] ^ tmp
            a[k + j] = a[k + j] ^ lax.shift_left(tmp, jnp.int32(j))
            k = (k + j + 1) & ~j
        j >>= 1
        m = (m ^ (m << j)) & 0xFFFFFFFF
    return a


def _fold_rows(x, op):
    x = x.reshape(x.shape[0] // SUBLANES, SUBLANES, x.shape[1])
    while x.shape[0] > 1:
        half = x.shape[0] // 2
        x = op(x[:half], x[half:])
    return x[0]


def _proj_kernel(h_ref, pos_ref, gmix_ref, wmain_ref, wt_ref, gcq_ref, wqupt_ref, gq_ref, gk_ref, gik_ref, invf_ref,
                 aqp_ref, iqt_ref, ak_ref, ik_ref, avt_ref, gt_ref, mq_ref, mk_ref, mv_ref, mo_ref):
    x = h_ref[...]
    ms = jnp.mean(x * x, axis=-1, keepdims=True)
    xn = (x * lax.rsqrt(ms + EPS) * gmix_ref[...]).astype(bf16)
    proj = _dot(xn, wmain_ref[...])
    pt = _dot_nt(wt_ref[...], xn)

    ang = invf_ref[...] * pos_ref[...].astype(f32)
    cos_t = jnp.cos(ang)
    sin_t = jnp.sin(ang)

    def rope_t(blk):
        x1 = blk[0:ROT_HALF]
        x2 = blk[ROT_HALF:ROT_DIM]
        return jnp.concatenate([x1 * cos_t - x2 * sin_t, x2 * cos_t + x1 * sin_t, blk[ROT_DIM:]], axis=0)

    def norm_t(blk, g_ref):
        return blk * lax.rsqrt(jnp.mean(blk * blk, axis=0, keepdims=True) + EPS) * g_ref[...]

    def head(a, j):
        return a[j * HEAD_DIM:(j + 1) * HEAD_DIM]

    cq = proj[:, OFF_CQ:OFF_CQ + Q_RANK]
    cqn = (cq * lax.rsqrt(jnp.mean(cq * cq, axis=-1, keepdims=True) + EPS) * gcq_ref[...]).astype(bf16)
    qqt = _dot_nt(wqupt_ref[...], cqn)
    tm = x.shape[0]

    def sq_norm_rows(blk_bf16):
        v = blk_bf16.astype(f32)
        return jnp.sum(v * v, axis=0, keepdims=True)

    zero_h = jnp.zeros((HEAD_DIM, tm), bf16)
    q_blocks, qn2 = [], []
    for j in range(ATT_HEADS):
        qb = (rope_t(norm_t(head(qqt, j), gq_ref)) * (ATT_SCALE * LOG2E)).astype(bf16)
        qn2.append(sq_norm_rows(qb))
        q_blocks += [qb if g == j // GQA else zero_h for g in range(ATT_KV_HEADS)]
    aqp_ref[...] = jnp.concatenate(q_blocks, axis=0)
    iqt = jnp.concatenate([rope_t(head(qqt, ATT_HEADS + j)) for j in range(IDX_HEADS)], axis=0)
    iqt_ref[...] = iqt.astype(bf16)

    akt = jnp.concatenate([rope_t(norm_t(head(pt, j), gk_ref)) for j in range(ATT_KV_HEADS)], axis=0)
    akb = akt.astype(bf16)
    kn2 = [sq_norm_rows(head(akb, g)) for g in range(ATT_KV_HEADS)]
    ak_ref[...] = akt.T.astype(bf16)
    ikt = rope_t(norm_t(pt[ROW_IK:ROW_IK + IDX_DIM], gik_ref))
    ikt = jnp.concatenate([ikt, jnp.zeros((LANES - IDX_DIM, tm), f32)], axis=0)
    ik_ref[...] = ikt.T.astype(bf16)
    avt = pt[ROW_AV:ROW_AV + KV_W].astype(bf16)
    ones = jnp.ones((V_AUG - HEAD_DIM, tm), bf16)
    avaug = jnp.concatenate([blk for g in range(ATT_KV_HEADS) for blk in (head(avt, g), ones)], axis=0)
    tk = avt_ref.shape[2]
    for j in range(avt_ref.shape[0]):
        avt_ref[j] = avaug[:, j * tk:(j + 1) * tk]
    gt_ref[...] = jnp.concatenate(
        [pt[ROW_GT:ROW_GT + GT_PROJ]] + qn2 + kn2
        + [jnp.zeros((GT_ROWS - GT_KN2 - ATT_KV_HEADS, tm), f32)], axis=0)

    mq_ref[...] = proj[:, OFF_MQ:OFF_MQ + ML_W]
    mk_ref[...] = proj[:, OFF_MK:OFF_MK + ML_W]
    mv_ref[...] = proj[:, OFF_MV:OFF_MV + ML_W].astype(bf16)
    mo_ref[...] = proj[:, OFF_MO:OFF_MO + ML_W]


def _proj_call(h, pos, gmix, wmain, wt, gcq, wqupt, gq, gk, gik, invf, tk):
    n = h.shape[0]
    tm = min(PROJ_TM, n)
    grid = (n // tm,)

    def tok(w):
        return pl.BlockSpec((tm, w), lambda i: (i, 0))

    def tok_t(r):
        return pl.BlockSpec((r, tm), lambda i: (0, i))

    def full(a):
        return pl.BlockSpec(a.shape, lambda i: (0,) * a.ndim)

    out_shape = [
        jax.ShapeDtypeStruct((ATT_HEADS * Q_PAD, n), bf16),
        jax.ShapeDtypeStruct((IDX_W, n), bf16),
        jax.ShapeDtypeStruct((n, KV_W), bf16),
        jax.ShapeDtypeStruct((n, LANES), bf16),
        jax.ShapeDtypeStruct((n // tk, ATT_KV_HEADS * V_AUG, tk), bf16),
        jax.ShapeDtypeStruct((GT_ROWS, n), f32),
        jax.ShapeDtypeStruct((n, ML_W), f32),
        jax.ShapeDtypeStruct((n, ML_W), f32),
        jax.ShapeDtypeStruct((n, ML_W), bf16),
        jax.ShapeDtypeStruct((n, ML_W), f32),
    ]
    out_specs = [tok_t(ATT_HEADS * Q_PAD), tok_t(IDX_W), tok(KV_W), tok(LANES),
                 pl.BlockSpec((tm // tk, ATT_KV_HEADS * V_AUG, tk), lambda i: (i, 0, 0)), tok_t(GT_ROWS),
                 tok(ML_W), tok(ML_W), tok(ML_W), tok(ML_W)]
    in_specs = [tok(D_MODEL), tok_t(1)] + [full(a) for a in (gmix, wmain, wt, gcq, wqupt, gq, gk, gik, invf)]
    return pl.pallas_call(
        _proj_kernel, grid=grid, in_specs=in_specs, out_specs=out_specs, out_shape=out_shape,
        compiler_params=pltpu.CompilerParams(dimension_semantics=("parallel",), vmem_limit_bytes=VMEM_LIMIT),
        name="proj",
    )(h, pos, gmix, wmain, wt, gcq, wqupt, gq, gk, gik, invf)


def _dsa_kernel(aqp_ref, iqt_ref, gt_ref, gtk_ref, ak_ref, ik_ref, avt_ref, o_ref,
                sc_ref, plane_ref, s_ref, cut_ref, m_ref, acc_ref, *, n_sel, t):
    qi = pl.program_id(1)
    nk = qi + 1
    k_loc = lax.broadcasted_iota(i32, (t, t), 0)
    q_pos = lax.broadcasted_iota(i32, (t, t), 1) + qi * t
    w_idx = gt_ref[GT_IW:GT_IW + IDX_HEADS, :] * IDX_SCALE

    def for_each_chunk(chunk_work):
        def pair(i, carry):
            chunk_work(2 * i)
            chunk_work(2 * i + 1)
            return carry
        lax.fori_loop(0, nk // 2, pair, 0)

        @pl.when(nk % 2 == 1)
        def _():
            chunk_work(nk - 1)

    k_max2 = jnp.max(gtk_ref[GT_KN2:GT_KN2 + ATT_KV_HEADS, :], axis=1, keepdims=True)
    bound = jnp.concatenate(
        [jnp.sqrt(gt_ref[GT_QN2 + hh:GT_QN2 + hh + 1, :] * k_max2[hh // GQA:hh // GQA + 1, :])
         for hh in range(ATT_HEADS)], axis=0) * BOUND_SLACK + (BOUND_SLACK - 1.0)

    @pl.when(qi == 0)
    def _():
        plane_ref[...] = jnp.zeros(plane_ref.shape, i32)

    def score_chunk(kc):
        off = pl.multiple_of(kc * t, t)
        ikc = ik_ref[pl.ds(off, t), 0:IDX_DIM]
        s = jnp.zeros((t, t), f32)
        for hh in range(IDX_HEADS):
            lg = _dot(ikc, iqt_ref[hh * IDX_DIM:(hh + 1) * IDX_DIM, :])
            s = s + w_idx[hh:hh + 1, :] * jnp.maximum(lg, 0.0)
        causal = k_loc + kc * t <= q_pos
        sc_ref[kc] = jnp.where(causal, s, -jnp.inf)
        bits = pltpu.bitcast(s, i32)
        key = bits ^ ((bits >> 31) & 0x7FFFFFFF)
        key = jnp.where(bits == INT_MIN, 0, key)
        key = jnp.where(causal, key, INT_MIN)
        u = key ^ INT_MIN
        planes = _bit_transpose32([u[SUBLANES * i:SUBLANES * (i + 1), :] for i in range(KEY_BITS)])
        for b in range(KEY_BITS):
            plane_ref[b, kc] = planes[b]
        rows = ak_ref[pl.ds(off, t), :]
        for hh in range(ATT_HEADS):
            s_ref[hh, kc] = _dot(rows, aqp_ref[hh * Q_PAD:(hh + 1) * Q_PAD, :])

    for_each_chunk(score_chunk)

    def popcount_rows(words):
        pc = lax.population_count(words)
        return jnp.sum(jnp.sum(pc, axis=0).astype(f32), axis=0, keepdims=True)

    def bit_body(it, carry):
        t_u, above, eq = carry
        ones = eq & plane_ref[it]
        c1 = popcount_rows(ones)
        ok = above + c1 >= n_sel
        t_u = jnp.where(ok, t_u | lax.shift_left(jnp.int32(1), KEY_BITS - 1 - it), t_u)
        return t_u, jnp.where(ok, above, above + c1), jnp.where(ok, ones, eq ^ ones)

    n_chunks = plane_ref.shape[1]
    word_chunk = lax.broadcasted_iota(i32, (n_chunks, SUBLANES, t), 0)
    eq0 = jnp.where(word_chunk < nk, -1, 0)
    t_u, above, eq = lax.fori_loop(0, KEY_BITS, bit_body,
                                   (jnp.zeros((1, t), i32), jnp.zeros((1, t), f32), eq0))
    cnt_t = above + popcount_rows(eq)
    t_s = t_u ^ INT_MIN

    cut_ref[...] = jnp.full((1, t), 2 ** 30, i32)

    @pl.when(jnp.max(cnt_t) > n_sel)
    def _():
        need = n_sel - above
        first_pos = word_chunk * t + lax.broadcasted_iota(i32, (n_chunks, SUBLANES, t), 1)

        def ties_below(bound_pos):
            n_top = jnp.clip((bound_pos - first_pos + (SUBLANES - 1)) >> 3, 0, KEY_BITS)
            top = lax.shift_right_arithmetic(jnp.full(n_top.shape, INT_MIN, i32), jnp.maximum(n_top - 1, 0))
            return popcount_rows(eq & jnp.where(n_top > 0, top, 0))

        c = jnp.zeros((1, t), i32)
        n_bits = max(1, int(ak_ref.shape[0] - 1).bit_length())
        for b in range(n_bits, -1, -1):
            cand = c | (1 << b)
            c = jnp.where(ties_below(cand) <= need, cand, c)
        cut_ref[...] = c

    def over_chunks(stat_fn, inits, ops):
        def body(kc, parts):
            vals = stat_fn(sc_ref[kc], k_loc + kc * t)
            return tuple(op(p, _fold_rows(v, op)) for p, v, op in zip(parts, vals, ops))
        parts = lax.fori_loop(0, nk, body, tuple(jnp.full((SUBLANES, t), v, f32) for v in inits))
        reducers = {jnp.add: jnp.sum, jnp.minimum: jnp.min, jnp.maximum: jnp.max}
        return tuple(reducers[op](p, axis=0, keepdims=True) for p, op in zip(parts, ops))

    def as_f32(m):
        return jnp.where(m, 1.0, 0.0)

    def counts(thr, cut):
        return over_chunks(lambda sc, idx: (as_f32(sc > thr), as_f32(sc == thr), as_f32((sc == thr) & (idx < cut))),
                           (0.0, 0.0, 0.0), (jnp.add, jnp.add, jnp.add))

    tb = jnp.where(t_s < 0, t_s ^ 0x7FFFFFFF, t_s)
    thr0 = jnp.where(t_s == INT_MIN, -jnp.inf, pltpu.bitcast(tb, f32))
    cut0 = cut_ref[...]

    def thr_is_off(n_above, n_equal):
        return (n_above >= n_sel) | (n_above + n_equal < n_sel)

    def polish_cond(state):
        _, n_above, n_equal, _ = state
        return jnp.max(as_f32(thr_is_off(n_above, n_equal))) > 0.0

    def polish_body(state):
        thr, n_above, n_equal, _ = state
        up, down = over_chunks(lambda sc, idx: (jnp.where(sc > thr, sc, jnp.inf), jnp.where(sc < thr, sc, -jnp.inf)),
                               (jnp.inf, -jnp.inf), (jnp.minimum, jnp.maximum))
        thr = jnp.where(n_above >= n_sel, up, jnp.where(n_above + n_equal < n_sel, down, thr))
        return (thr,) + counts(thr, cut0)

    thr, n_above, n_equal, n_kept = lax.while_loop(polish_cond, polish_body, (thr0,) + counts(thr0, cut0))
    need = n_sel - n_above
    cut_is_off = n_kept != jnp.minimum(n_equal, need)

    @pl.when(jnp.max(as_f32(cut_is_off)) > 0.0)
    def _():
        c = jnp.zeros((1, t), i32)
        n_bits = max(1, int(ak_ref.shape[0] - 1).bit_length())
        for b in range(n_bits, -1, -1):
            cand = c | (1 << b)
            (below,) = over_chunks(lambda sc, idx: (as_f32((sc == thr) & (idx < cand)),), (0.0,), (jnp.add,))
            c = jnp.where(below <= need, cand, c)
        cut_ref[...] = c

    cut = cut_ref[...]

    def selected(kc):
        sc = sc_ref[kc]
        idx = k_loc + kc * t
        return ((sc > thr) | ((sc == thr) & (idx < cut))) & (idx <= q_pos)

    def add_value_products(kc, hh, p_bf16):
        g = hh // GQA
        acc_ref[hh] += _dot(avt_ref[kc, g * V_AUG:(g + 1) * V_AUG, :], p_bf16)

    acc_ref[...] = jnp.zeros(acc_ref.shape, f32)
    shift_is_safe = jnp.max(bound) <= BOUND_LIMIT

    @pl.when(shift_is_safe)
    def _():
        def att_chunk(kc):
            keep = jnp.where(selected(kc), 1.0, 0.0).astype(bf16)
            for hh in range(ATT_HEADS):
                add_value_products(kc, hh, jnp.exp2(s_ref[hh, kc]).astype(bf16) * keep)
        for_each_chunk(att_chunk)

    @pl.when(jnp.logical_not(shift_is_safe))
    def _():
        def max_body(kc, mx):
            bias = jnp.where(selected(kc), 0.0, NEG_BIG)
            rows = [jnp.max(_fold_rows(s_ref[hh, kc] + bias, jnp.maximum), axis=0, keepdims=True)
                    for hh in range(ATT_HEADS)]
            return jnp.maximum(mx, jnp.concatenate(rows, axis=0))
        m_ref[...] = lax.fori_loop(0, nk, max_body, jnp.full((ATT_HEADS, t), NEG_BIG, f32))

        def att_body(kc, carry):
            bias = jnp.where(selected(kc), 0.0, NEG_BIG)
            for hh in range(ATT_HEADS):
                p = jnp.exp2(s_ref[hh, kc] + bias - m_ref[hh:hh + 1, :])
                add_value_products(kc, hh, p.astype(bf16))
            return carry
        lax.fori_loop(0, nk, att_body, 0)

    att_t = jnp.concatenate(
        [acc_ref[hh, 0:HEAD_DIM, :] / acc_ref[hh, HEAD_DIM:HEAD_DIM + 1, :] for hh in range(ATT_HEADS)], axis=0)
    o_ref[...] = att_t.T.astype(o_ref.dtype)


def _dsa_call(aqp, iqt, gt, ak, ik, avt, n_sel, b, s, t):
    nq = s // t

    def qblk_t(r):
        return pl.BlockSpec((r, t), lambda bi, qi: (0, bi * nq + qi))

    def kblk(w):
        return pl.BlockSpec((None, s, w), lambda bi, qi: (bi, 0, 0))

    return pl.pallas_call(
        functools.partial(_dsa_kernel, n_sel=n_sel, t=t),
        grid=(b, nq),
        in_specs=[qblk_t(ATT_HEADS * Q_PAD), qblk_t(IDX_W), qblk_t(GT_ROWS),
                  pl.BlockSpec((GT_ROWS, s), lambda bi, qi: (0, bi)), kblk(KV_W), kblk(LANES),
                  pl.BlockSpec((nq, ATT_KV_HEADS * V_AUG, t), lambda bi, qi: (bi, 0, 0))],
        out_specs=pl.BlockSpec((None, t, ATT_W), lambda bi, qi: (bi, qi, 0)),
        out_shape=jax.ShapeDtypeStruct((b, s, ATT_W), bf16),
        scratch_shapes=[
            pltpu.VMEM((nq, t, t), f32),
            pltpu.VMEM((KEY_BITS, nq, SUBLANES, t), i32),
            pltpu.VMEM((ATT_HEADS, nq, t, t), f32),
            pltpu.VMEM((1, t), i32),
            pltpu.VMEM((ATT_HEADS, t), f32),
            pltpu.VMEM((ATT_HEADS, V_AUG, t), f32),
        ],
        compiler_params=pltpu.CompilerParams(dimension_semantics=("parallel", "arbitrary"),
                                             vmem_limit_bytes=VMEM_LIMIT),
        name="dsa",
    )(aqp, iqt, gt, gt, ak, ik, avt)


def _mlstm_kernel(mq_ref, mk_ref, mv_ref, mo_ref, gt_ref, gb_ref, cwq_ref, cwk_ref, cbq_ref, cbk_ref, gmh_ref,
                  o_ref, *, chunk):
    hd = pl.program_id(1)
    seq = mq_ref.shape[0]
    n_chunks = seq // chunk
    row = lax.broadcasted_iota(i32, (chunk, chunk), 0)
    col = lax.broadcasted_iota(i32, (chunk, chunk), 1)
    tril = col <= row
    eye = col == row
    row_t = lax.broadcasted_iota(i32, (SUBLANES, ML_DIM), 0)

    def conv_silu(x_ref, w_ref, b_ref, c):
        t0 = c * chunk
        cur = x_ref[t0:t0 + chunk, :]
        acc = cur * w_ref[CONV_W - 1:CONV_W, :] + b_ref[...]
        for j in range(1, CONV_W):
            if c > 0:
                shifted = x_ref[t0 - j:t0 - j + chunk, :]
            else:
                rolled = pltpu.roll(cur, j, 0)
                first = jnp.where(row_t >= j, rolled[0:SUBLANES], 0.0)
                shifted = jnp.concatenate([first, rolled[SUBLANES:]], axis=0)
            acc = acc + shifted * w_ref[CONV_W - 1 - j:CONV_W - j, :]
        return acc * _sigmoid(acc)

    c_state = jnp.zeros((ML_DIM, ML_DIM), f32)
    n_state = jnp.zeros((1, ML_DIM), f32)
    m_state = jnp.zeros((1, 1), f32)
    li_all = gt_ref[pl.ds(GT_MI + hd, 1), :] + gb_ref[pl.ds(hd, 1), :]
    f_all = gt_ref[pl.ds(GT_MF + hd, 1), :] + gb_ref[pl.ds(ML_HEADS + hd, 1), :]
    lf_all = -(jnp.maximum(-f_all, 0.0) + jnp.log1p(jnp.exp(-jnp.abs(f_all))))
    for c in range(n_chunks):
        t0 = c * chunk
        q = conv_silu(mq_ref, cwq_ref, cbq_ref, c) * (ML_DIM ** -0.5)
        k = conv_silu(mk_ref, cwk_ref, cbk_ref, c)
        qb = q.astype(bf16)
        kb = k.astype(bf16)
        vb = mv_ref[t0:t0 + chunk, :]
        li_row = li_all[:, t0:t0 + chunk]
        lf_row = lf_all[:, t0:t0 + chunk]
        b_col = jnp.sum(jnp.where(tril, lf_row, 0.0), axis=1, keepdims=True)
        b_row = jnp.sum(jnp.where(eye, b_col, 0.0), axis=0, keepdims=True)
        li_col = jnp.sum(jnp.where(eye, li_row, 0.0), axis=1, keepdims=True)
        b_last = jnp.sum(lf_row, axis=1, keepdims=True)

        log_d = jnp.where(tril, b_col - b_row + li_row, -jnp.inf)
        inter = b_col + m_state
        m_t = jnp.maximum(inter, jnp.max(log_d, axis=1, keepdims=True))
        dmat = jnp.exp(log_d - m_t)
        inter_w = jnp.exp(inter - m_t)
        qk = _dot_nt(qb, kb) * dmat
        num = inter_w * _dot(qb, c_state.astype(bf16)) + _dot(qk.astype(bf16), vb)
        den = inter_w * jnp.sum(q * n_state, axis=1, keepdims=True) + jnp.sum(qk, axis=1, keepdims=True)
        h_t = num / jnp.maximum(jnp.abs(den), jnp.exp(-m_t))

        log_g = b_last - b_col + li_col
        m_new = jnp.maximum(b_last + m_state, jnp.max(log_g, axis=0, keepdims=True))
        g = jnp.exp(log_g - m_new)
        decay = jnp.exp(b_last + m_state - m_new)
        gk = g * k
        c_state = decay * c_state + _dot(gk.T.astype(bf16), vb)
        n_state = decay * n_state + jnp.sum(gk, axis=0, keepdims=True)
        m_state = m_new

        hn = h_t * lax.rsqrt(jnp.mean(h_t * h_t, axis=1, keepdims=True) + EPS) * gmh_ref[...]
        o_ref[t0:t0 + chunk, :] = (_sigmoid(mo_ref[t0:t0 + chunk, :]) * hn).astype(o_ref.dtype)


def _mlstm_call(mq, mk, mv, mo, gt, gb, conv_w, conv_b, gmh):
    b, s, _ = mq.shape
    chunk = min(ML_CHUNK, s)

    def head_blk():
        return pl.BlockSpec((None, s, ML_DIM), lambda bi, hi: (bi, 0, hi))

    return pl.pallas_call(
        functools.partial(_mlstm_kernel, chunk=chunk),
        grid=(b, ML_HEADS),
        in_specs=[head_blk(), head_blk(), head_blk(), head_blk(),
                  pl.BlockSpec((GT_ROWS, s), lambda bi, hi: (0, bi)),
                  pl.BlockSpec((2 * ML_HEADS, 1), lambda bi, hi: (0, 0)),
                  pl.BlockSpec((CONV_W, ML_DIM), lambda bi, hi: (0, hi)),
                  pl.BlockSpec((CONV_W, ML_DIM), lambda bi, hi: (0, ML_HEADS + hi)),
                  pl.BlockSpec((1, ML_DIM), lambda bi, hi: (0, hi)),
                  pl.BlockSpec((1, ML_DIM), lambda bi, hi: (0, ML_HEADS + hi)),
                  pl.BlockSpec((None, 1, ML_DIM), lambda bi, hi: (hi, 0, 0))],
        out_specs=head_blk(),
        out_shape=jax.ShapeDtypeStruct((b, s, ML_W), bf16),
        compiler_params=pltpu.CompilerParams(dimension_semantics=("parallel", "parallel"),
                                             vmem_limit_bytes=VMEM_LIMIT),
        name="mlstm",
    )(mq, mk, mv, mo, gt, gb, conv_w, conv_w, conv_b, conv_b, gmh)


def _mix_kernel(h_ref, att_ref, hm_ref, p_ref, wo_ref, gmlp_ref, w1_ref, w2_ref, gple_ref, wg_ref, bg_ref, wp_ref,
                o_ref):
    def rms(v, g_ref):
        return (v * lax.rsqrt(jnp.mean(v * v, axis=-1, keepdims=True) + EPS) * g_ref[...]).astype(bf16)

    mixed = _dot(att_ref[...], wo_ref[0:ATT_W, :]) + _dot(hm_ref[...], wo_ref[ATT_W:ATT_W + ML_W, :])
    h1 = h_ref[...] + mixed
    xn = rms(h1, gmlp_ref)
    mlp = None
    for f in range(D_FF // FF_CHUNK):
        u = jnp.maximum(_dot(xn, w1_ref[:, f * FF_CHUNK:(f + 1) * FF_CHUNK]), 0.0)
        part = _dot((u * u).astype(bf16), w2_ref[f * FF_CHUNK:(f + 1) * FF_CHUNK, :])
        mlp = part if mlp is None else mlp + part
    h2 = h1 + mlp
    gate = _sigmoid(_dot(rms(h2, gple_ref), wg_ref[...]) + bg_ref[...])
    o_ref[...] = h2 + gate * _dot(p_ref[...].astype(bf16), wp_ref[...])


def _mix_call(h, att, hm, p, wo, gmlp, w1, w2, gple, wg, bg, wp):
    n = h.shape[0]
    tm = min(MIX_TM, n)

    def tok(w):
        return pl.BlockSpec((tm, w), lambda i: (i, 0))

    def full(a):
        return pl.BlockSpec(a.shape, lambda i: (0,) * a.ndim, pipeline_mode=pl.Buffered(1))

    return pl.pallas_call(
        _mix_kernel, grid=(n // tm,),
        in_specs=[tok(D_MODEL), tok(ATT_W), tok(ML_W), tok(PLE_DIM)]
                 + [full(a) for a in (wo, gmlp, w1, w2, gple, wg, bg, wp)],
        out_specs=tok(D_MODEL),
        out_shape=jax.ShapeDtypeStruct((n, D_MODEL), f32),
        compiler_params=pltpu.CompilerParams(dimension_semantics=("parallel",), vmem_limit_bytes=VMEM_LIMIT),
        name="mix",
    )(h, att, hm, p, wo, gmlp, w1, w2, gple, wg, bg, wp)


def kernel(x, p, positions, g_mix, w_in, g_cq, w_q_up, w_iq_up, g_qn, g_kn, g_ik, conv_w, conv_b, i_bias, f_bias,
           g_mh, w_out, g_mlp, w_ff1, w_ff2, g_ple, w_ple_gate, b_ple_gate, w_ple):
    b, s, d = x.shape
    n = b * s
    depth = p.shape[0]
    n_sel = min(TOPK_MAX, s // 4)
    t = min(DSA_T, s)
    h = x.reshape(n, d)
    pos = positions.reshape(1, n).astype(i32)
    invf = (ROPE_THETA ** (-(jnp.arange(ROT_HALF, dtype=f32) * 2.0) / ROT_DIM))[:, None]

    split = np.cumsum(IN_SIZES)[:-1].tolist()
    for i in range(depth):
        c_q, a_k, a_v, i_k, i_w, m_q, m_k, m_v, m_o, m_i, m_f = jnp.split(w_in[i], split, axis=1)
        wmain = jnp.concatenate([c_q, m_q, m_k, m_v, m_o], axis=1).astype(bf16)
        wt = jnp.concatenate([a_k, i_k, a_v, i_w, m_i, m_f], axis=1).T.astype(bf16)
        wqupt = jnp.concatenate([w_q_up[i], w_iq_up[i]], axis=1).T.astype(bf16)
        aqt, iqt, ak, ik, avt, gt, mq, mk, mv, mo = _proj_call(
            h, pos, g_mix[i][None, :], wmain, wt, g_cq[i][None, :], wqupt,
            g_qn[i][:, None], g_kn[i][:, None], g_ik[i][:, None], invf, t)

        def b3(a):
            return a.reshape(b, s, a.shape[-1])

        att = _dsa_call(aqt, iqt, gt, b3(ak), b3(ik), avt, n_sel, b, s, t)
        gb = jnp.concatenate([i_bias[i], f_bias[i]])[:, None]
        hm = _mlstm_call(b3(mq), b3(mk), b3(mv), b3(mo), gt, gb, conv_w[i], conv_b[i][None, :],
                         g_mh[i][:, None, :])
        h = _mix_call(h, att.reshape(n, ATT_W), hm.reshape(n, ML_W), p[i].reshape(n, PLE_DIM),
                      w_out[i].astype(bf16), g_mlp[i][None, :], w_ff1[i].astype(bf16), w_ff2[i].astype(bf16),
                      g_ple[i][None, :], w_ple_gate[i].astype(bf16), b_ple_gate[i][None, :], w_ple[i].astype(bf16))
    return h.reshape(b, s, d)
```

```python
import functools

import numpy as np
import jax
import jax.numpy as jnp
from jax import lax
from jax.experimental import pallas as pl
from jax.experimental.pallas import tpu as pltpu

D_MODEL = 1024
PLE_DIM = 256
ATT_HEADS = 8
ATT_KV_HEADS = 2
HEAD_DIM = 64
Q_RANK = 256
IDX_HEADS = 8
IDX_DIM = 64
TOPK_MAX = 256
ML_HEADS = 4
ML_DIM = 128
CONV_W = 4
D_FF = 4 * D_MODEL
ROPE_THETA = 500000.0
ROT_DIM = HEAD_DIM // 4
ROT_HALF = ROT_DIM // 2
EPS = 1e-6

ATT_W = ATT_HEADS * HEAD_DIM
KV_W = ATT_KV_HEADS * HEAD_DIM
ML_W = ML_HEADS * ML_DIM
IDX_W = IDX_HEADS * IDX_DIM
IN_SIZES = (Q_RANK, KV_W, KV_W, IDX_DIM, IDX_HEADS, ML_W, ML_W, ML_W, ML_W, ML_HEADS, ML_HEADS)
IDX_SCALE = (IDX_HEADS ** -0.5) * (IDX_DIM ** -0.5)
ATT_SCALE = HEAD_DIM ** -0.5
GQA = ATT_HEADS // ATT_KV_HEADS

LANES = 128
SUBLANES = 8
OFF_CQ = 0
OFF_MQ = Q_RANK
OFF_MK, OFF_MV, OFF_MO = OFF_MQ + ML_W, OFF_MQ + 2 * ML_W, OFF_MQ + 3 * ML_W
MAIN_W = OFF_MQ + 4 * ML_W
ROW_AK, ROW_IK, ROW_AV = 0, KV_W, KV_W + IDX_DIM
ROW_GT = ROW_AV + KV_W
GT_IW, GT_MI, GT_MF = 0, IDX_HEADS, IDX_HEADS + ML_HEADS
GT_PROJ = IDX_HEADS + 2 * ML_HEADS
GT_QN2 = GT_PROJ
GT_KN2 = GT_QN2 + ATT_HEADS
GT_ROWS = 32
T_ROWS = ROW_GT + GT_PROJ
Q_PAD = 2 * HEAD_DIM
V_AUG = HEAD_DIM + 16
KEY_BITS = 32
LOG2E = 1.4426950408889634
BOUND_SLACK = 1.001
BOUND_LIMIT = 40.0

PROJ_TM = 512
MIX_TM = 512
FF_CHUNK = 1024
DSA_T = 256
ML_CHUNK = 256
NEG_BIG = -1e30
VMEM_LIMIT = 56 * 1024 * 1024

_NT = (((1,), (1,)), ((), ()))

f32 = jnp.float32
bf16 = jnp.bfloat16
i32 = jnp.int32
INT_MIN = -2 ** 31


def _dot(a, b):
    return jnp.dot(a, b, preferred_element_type=f32)


def _dot_nt(a, b):
    return lax.dot_general(a, b, _NT, preferred_element_type=f32)


def _sigmoid(x):
    return 0.5 * jnp.tanh(0.5 * x) + 0.5


def _bit_transpose32(words):
    a = list(words)
    j, m = 16, 0x0000FFFF
    while j:
        mask = int(np.array(m, np.uint32).view(np.int32))
        k = 0
        while k < 32:
            tmp = (a[k] ^ lax.shift_right_logical(a[k + j], jnp.int32(j))) & mask
            a[k] = a[k] ^ tmp
            a[k + j] = a[k + j] ^ lax.shift_left(tmp, jnp.int32(j))
            k = (k + j + 1) & ~j
        j >>= 1
        m = (m ^ (m << j)) & 0xFFFFFFFF
    return a


def _fold_rows(x, op):
    x = x.reshape(x.shape[0] // SUBLANES, SUBLANES, x.shape[1])
    while x.shape[0] > 1:
        half = x.shape[0] // 2
        x = op(x[:half], x[half:])
    return x[0]


def _proj_kernel(h_ref, pos_ref, gmix_ref, wmain_ref, wt_ref, gcq_ref, wqupt_ref, gq_ref, gk_ref, gik_ref, invf_ref,
                 aqp_ref, iqt_ref, ak_ref, ik_ref, avt_ref, gt_ref, mq_ref, mk_ref, mv_ref, mo_ref):
    x = h_ref[...]
    ms = jnp.mean(x * x, axis=-1, keepdims=True)
    xn = (x * lax.rsqrt(ms + EPS) * gmix_ref[...]).astype(bf16)
    proj = _dot(xn, wmain_ref[...])
    pt = _dot_nt(wt_ref[...], xn)

    ang = invf_ref[...] * pos_ref[...].astype(f32)
    cos_t = jnp.cos(ang)
    sin_t = jnp.sin(ang)

    def rope_t(blk):
        x1 = blk[0:ROT_HALF]
        x2 = blk[ROT_HALF:ROT_DIM]
        return jnp.concatenate([x1 * cos_t - x2 * sin_t, x2 * cos_t + x1 * sin_t, blk[ROT_DIM:]], axis=0)

    def norm_t(blk, g_ref):
        return blk * lax.rsqrt(jnp.mean(blk * blk, axis=0, keepdims=True) + EPS) * g_ref[...]

    def head(a, j):
        return a[j * HEAD_DIM:(j + 1) * HEAD_DIM]

    cq = proj[:, OFF_CQ:OFF_CQ + Q_RANK]
    cqn = (cq * lax.rsqrt(jnp.mean(cq * cq, axis=-1, keepdims=True) + EPS) * gcq_ref[...]).astype(bf16)
    qqt = _dot_nt(wqupt_ref[...], cqn)
    tm = x.shape[0]

    def sq_norm_rows(blk_bf16):
        v = blk_bf16.astype(f32)
        return jnp.sum(v * v, axis=0, keepdims=True)

    zero_h = jnp.zeros((HEAD_DIM, tm), bf16)
    q_blocks, qn2 = [], []
    for j in range(ATT_HEADS):
        qb = (rope_t(norm_t(head(qqt, j), gq_ref)) * (ATT_SCALE * LOG2E)).astype(bf16)
        qn2.append(sq_norm_rows(qb))
        q_blocks += [qb if g == j // GQA else zero_h for g in range(ATT_KV_HEADS)]
    aqp_ref[...] = jnp.concatenate(q_blocks, axis=0)
    iqt = jnp.concatenate([rope_t(head(qqt, ATT_HEADS + j)) for j in range(IDX_HEADS)], axis=0)
    iqt_ref[...] = iqt.astype(bf16)

    akt = jnp.concatenate([rope_t(norm_t(head(pt, j), gk_ref)) for j in range(ATT_KV_HEADS)], axis=0)
    akb = akt.astype(bf16)
    kn2 = [sq_norm_rows(head(akb, g)) for g in range(ATT_KV_HEADS)]
    ak_ref[...] = akt.T.astype(bf16)
    ikt = rope_t(norm_t(pt[ROW_IK:ROW_IK + IDX_DIM], gik_ref))
    ikt = jnp.concatenate([ikt, jnp.zeros((LANES - IDX_DIM, tm), f32)], axis=0)
    ik_ref[...] = ikt.T.astype(bf16)
    avt = pt[ROW_AV:ROW_AV + KV_W].astype(bf16)
    ones = jnp.ones((V_AUG - HEAD_DIM, tm), bf16)
    avaug = jnp.concatenate([blk for g in range(ATT_KV_HEADS) for blk in (head(avt, g), ones)], axis=0)
    tk = avt_ref.shape[2]
    for j in range(avt_ref.shape[0]):
        avt_ref[j] = avaug[:, j * tk:(j + 1) * tk]
    gt_ref[...] = jnp.concatenate(
        [pt[ROW_GT:ROW_GT + GT_PROJ]] + qn2 + kn2
        + [jnp.zeros((GT_ROWS - GT_KN2 - ATT_KV_HEADS, tm), f32)], axis=0)

    mq_ref[...] = proj[:, OFF_MQ:OFF_MQ + ML_W]
    mk_ref[...] = proj[:, OFF_MK:OFF_MK + ML_W]
    mv_ref[...] = proj[:, OFF_MV:OFF_MV + ML_W].astype(bf16)
    mo_ref[...] = proj[:, OFF_MO:OFF_MO + ML_W]


def _proj_call(h, pos, gmix, wmain, wt, gcq, wqupt, gq, gk, gik, invf, tk):
    n = h.shape[0]
    tm = min(PROJ_TM, n)
    grid = (n // tm,)

    def tok(w):
        return pl.BlockSpec((tm, w), lambda i: (i, 0))

    def tok_t(r):
        return pl.BlockSpec((r, tm), lambda i: (0, i))

    def full(a):
        return pl.BlockSpec(a.shape, lambda i: (0,) * a.ndim)

    out_shape = [
        jax.ShapeDtypeStruct((ATT_HEADS * Q_PAD, n), bf16),
        jax.ShapeDtypeStruct((IDX_W, n), bf16),
        jax.ShapeDtypeStruct((n, KV_W), bf16),
        jax.ShapeDtypeStruct((n, LANES), bf16),
        jax.ShapeDtypeStruct((n // tk, ATT_KV_HEADS * V_AUG, tk), bf16),
        jax.ShapeDtypeStruct((GT_ROWS, n), f32),
        jax.ShapeDtypeStruct((n, ML_W), f32),
        jax.ShapeDtypeStruct((n, ML_W), f32),
        jax.ShapeDtypeStruct((n, ML_W), bf16),
        jax.ShapeDtypeStruct((n, ML_W), f32),
    ]
    out_specs = [tok_t(ATT_HEADS * Q_PAD), tok_t(IDX_W), tok(KV_W), tok(LANES),
                 pl.BlockSpec((tm // tk, ATT_KV_HEADS * V_AUG, tk), lambda i: (i, 0, 0)), tok_t(GT_ROWS),
                 tok(ML_W), tok(ML_W), tok(ML_W), tok(ML_W)]
    in_specs = [tok(D_MODEL), tok_t(1)] + [full(a) for a in (gmix, wmain, wt, gcq, wqupt, gq, gk, gik, invf)]
    return pl.pallas_call(
        _proj_kernel, grid=grid, in_specs=in_specs, out_specs=out_specs, out_shape=out_shape,
        compiler_params=pltpu.CompilerParams(dimension_semantics=("parallel",), vmem_limit_bytes=VMEM_LIMIT),
        name="proj",
    )(h, pos, gmix, wmain, wt, gcq, wqupt, gq, gk, gik, invf)


def _dsa_kernel(aqp_ref, iqt_ref, gt_ref, gtk_ref, ak_ref, ik_ref, avt_ref, o_ref,
                sc_ref, plane_ref, s_ref, cut_ref, m_ref, acc_ref, kept_ref, *, n_sel, t):
    qi = pl.program_id(1)
    nk = qi + 1
    k_loc = lax.broadcasted_iota(i32, (t, t), 0)
    q_pos = lax.broadcasted_iota(i32, (t, t), 1) + qi * t
    w_idx = gt_ref[GT_IW:GT_IW + IDX_HEADS, :] * IDX_SCALE

    def for_each_chunk(chunk_work):
        def pair(i, carry):
            chunk_work(2 * i)
            chunk_work(2 * i + 1)
            return carry
        lax.fori_loop(0, nk // 2, pair, 0)

        @pl.when(nk % 2 == 1)
        def _():
            chunk_work(nk - 1)

    k_max2 = jnp.max(gtk_ref[GT_KN2:GT_KN2 + ATT_KV_HEADS, :], axis=1, keepdims=True)
    bound = jnp.concatenate(
        [jnp.sqrt(gt_ref[GT_QN2 + hh:GT_QN2 + hh + 1, :] * k_max2[hh // GQA:hh // GQA + 1, :])
         for hh in range(ATT_HEADS)], axis=0) * BOUND_SLACK + (BOUND_SLACK - 1.0)

    @pl.when(qi == 0)
    def _():
        plane_ref[...] = jnp.zeros(plane_ref.shape, i32)

    def score_chunk(kc):
        off = pl.multiple_of(kc * t, t)
        ikc = ik_ref[pl.ds(off, t), 0:IDX_DIM]
        s = jnp.zeros((t, t), f32)
        for hh in range(IDX_HEADS):
            lg = _dot(ikc, iqt_ref[hh * IDX_DIM:(hh + 1) * IDX_DIM, :])
            s = s + w_idx[hh:hh + 1, :] * jnp.maximum(lg, 0.0)
        causal = k_loc + kc * t <= q_pos
        sc_ref[kc] = jnp.where(causal, s, -jnp.inf)
        bits = pltpu.bitcast(s, i32)
        key = bits ^ ((bits >> 31) & 0x7FFFFFFF)
        key = jnp.where(bits == INT_MIN, 0, key)
        key = jnp.where(causal, key, INT_MIN)
        u = key ^ INT_MIN
        planes = _bit_transpose32([u[SUBLANES * i:SUBLANES * (i + 1), :] for i in range(KEY_BITS)])
        for b in range(KEY_BITS):
            plane_ref[b, kc] = planes[b]
        rows = ak_ref[pl.ds(off, t), :]
        for hh in range(ATT_HEADS):
            s_ref[hh, kc] = _dot(rows, aqp_ref[hh * Q_PAD:(hh + 1) * Q_PAD, :])

    for_each_chunk(score_chunk)

    def popcount_rows(words):
        pc = lax.population_count(words)
        return jnp.sum(jnp.sum(pc, axis=0).astype(f32), axis=0, keepdims=True)

    def bit_body(it, carry):
        t_u, above, eq = carry
        ones = eq & plane_ref[it]
        c1 = popcount_rows(ones)
        ok = above + c1 >= n_sel
        t_u = jnp.where(ok, t_u | lax.shift_left(jnp.int32(1), KEY_BITS - 1 - it), t_u)
        return t_u, jnp.where(ok, above, above + c1), jnp.where(ok, ones, eq ^ ones)

    n_chunks = plane_ref.shape[1]
    word_chunk = lax.broadcasted_iota(i32, (n_chunks, SUBLANES, t), 0)
    eq0 = jnp.where(word_chunk < nk, -1, 0)
    t_u, above, eq = lax.fori_loop(0, KEY_BITS, bit_body,
                                   (jnp.zeros((1, t), i32), jnp.zeros((1, t), f32), eq0))
    cnt_t = above + popcount_rows(eq)
    t_s = t_u ^ INT_MIN

    cut_ref[...] = jnp.full((1, t), 2 ** 30, i32)

    @pl.when(jnp.max(cnt_t) > n_sel)
    def _():
        need = n_sel - above
        first_pos = word_chunk * t + lax.broadcasted_iota(i32, (n_chunks, SUBLANES, t), 1)

        def ties_below(bound_pos):
            n_top = jnp.clip((bound_pos - first_pos + (SUBLANES - 1)) >> 3, 0, KEY_BITS)
            top = lax.shift_right_arithmetic(jnp.full(n_top.shape, INT_MIN, i32), jnp.maximum(n_top - 1, 0))
            return popcount_rows(eq & jnp.where(n_top > 0, top, 0))

        c = jnp.zeros((1, t), i32)
        n_bits = max(1, int(ak_ref.shape[0] - 1).bit_length())
        for b in range(n_bits, -1, -1):
            cand = c | (1 << b)
            c = jnp.where(ties_below(cand) <= need, cand, c)
        cut_ref[...] = c

    def as_f32(m):
        return jnp.where(m, 1.0, 0.0)

    def selected(kc, thr, cut):
        sc = sc_ref[kc]
        idx = k_loc + kc * t
        return ((sc > thr) | ((sc == thr) & (idx < cut))) & (idx <= q_pos)

    def add_value_products(kc, hh, p_bf16):
        g = hh // GQA
        acc_ref[hh] += _dot(avt_ref[kc, g * V_AUG:(g + 1) * V_AUG, :], p_bf16)

    shift_is_safe = jnp.max(bound) <= BOUND_LIMIT
    ones_rows = jnp.ones((kept_ref.shape[0], t), bf16)

    def attention(thr, cut, count_kept):
        acc_ref[...] = jnp.zeros(acc_ref.shape, f32)
        if count_kept:
            kept_ref[...] = jnp.zeros(kept_ref.shape, f32)

        @pl.when(shift_is_safe)
        def _():
            def att_chunk(kc):
                keep = as_f32(selected(kc, thr, cut)).astype(bf16)
                if count_kept:
                    kept_ref[...] += _dot(ones_rows, keep)
                for hh in range(ATT_HEADS):
                    add_value_products(kc, hh, jnp.exp2(s_ref[hh, kc]).astype(bf16) * keep)
            for_each_chunk(att_chunk)

        @pl.when(jnp.logical_not(shift_is_safe))
        def _():
            def max_body(kc, mx):
                bias = jnp.where(selected(kc, thr, cut), 0.0, NEG_BIG)
                rows = [jnp.max(_fold_rows(s_ref[hh, kc] + bias, jnp.maximum), axis=0, keepdims=True)
                        for hh in range(ATT_HEADS)]
                return jnp.maximum(mx, jnp.concatenate(rows, axis=0))
            m_ref[...] = lax.fori_loop(0, nk, max_body, jnp.full((ATT_HEADS, t), NEG_BIG, f32))

            def att_body(kc, carry):
                sel = selected(kc, thr, cut)
                if count_kept:
                    kept_ref[...] += _dot(ones_rows, as_f32(sel).astype(bf16))
                bias = jnp.where(sel, 0.0, NEG_BIG)
                for hh in range(ATT_HEADS):
                    p = jnp.exp2(s_ref[hh, kc] + bias - m_ref[hh:hh + 1, :])
                    add_value_products(kc, hh, p.astype(bf16))
                return carry
            lax.fori_loop(0, nk, att_body, 0)

    tb = jnp.where(t_s < 0, t_s ^ 0x7FFFFFFF, t_s)
    thr0 = jnp.where(t_s == INT_MIN, -jnp.inf, pltpu.bitcast(tb, f32))
    cut0 = cut_ref[...]
    attention(thr0, cut0, True)
    n_causal = (lax.broadcasted_iota(i32, (1, t), 1) + (qi * t + 1)).astype(f32)
    kept_is_off = kept_ref[0:1, :] != jnp.minimum(n_causal, float(n_sel))

    @pl.when(jnp.max(as_f32(kept_is_off)) > 0.0)
    def _():
        def over_chunks(stat_fn, inits, ops):
            def body(kc, parts):
                vals = stat_fn(sc_ref[kc], k_loc + kc * t)
                return tuple(op(p, _fold_rows(v, op)) for p, v, op in zip(parts, vals, ops))
            parts = lax.fori_loop(0, nk, body, tuple(jnp.full((SUBLANES, t), v, f32) for v in inits))
            reducers = {jnp.add: jnp.sum, jnp.minimum: jnp.min, jnp.maximum: jnp.max}
            return tuple(reducers[op](p, axis=0, keepdims=True) for p, op in zip(parts, ops))

        def counts(thr):
            return over_chunks(
                lambda sc, idx: (as_f32(sc > thr), as_f32(sc == thr), as_f32((sc == thr) & (idx < cut0))),
                (0.0, 0.0, 0.0), (jnp.add, jnp.add, jnp.add))

        def polish_cond(state):
            _, n_above, n_equal, _ = state
            return jnp.max(as_f32((n_above >= n_sel) | (n_above + n_equal < n_sel))) > 0.0

        def polish_body(state):
            thr, n_above, n_equal, _ = state
            up, down = over_chunks(
                lambda sc, idx: (jnp.where(sc > thr, sc, jnp.inf), jnp.where(sc < thr, sc, -jnp.inf)),
                (jnp.inf, -jnp.inf), (jnp.minimum, jnp.maximum))
            thr = jnp.where(n_above >= n_sel, up, jnp.where(n_above + n_equal < n_sel, down, thr))
            return (thr,) + counts(thr)

        thr, n_above, n_equal, n_kept = lax.while_loop(polish_cond, polish_body, (thr0,) + counts(thr0))
        need = n_sel - n_above
        cut_is_off = n_kept != jnp.minimum(n_equal, need)

        @pl.when(jnp.max(as_f32(cut_is_off)) > 0.0)
        def _():
            c = jnp.zeros((1, t), i32)
            n_bits = max(1, int(ak_ref.shape[0] - 1).bit_length())
            for b in range(n_bits, -1, -1):
                cand = c | (1 << b)
                (below,) = over_chunks(lambda sc, idx: (as_f32((sc == thr) & (idx < cand)),), (0.0,), (jnp.add,))
                c = jnp.where(below <= need, cand, c)
            cut_ref[...] = c

        attention(thr, cut_ref[...], False)

    att_t = jnp.concatenate(
        [acc_ref[hh, 0:HEAD_DIM, :] / acc_ref[hh, HEAD_DIM:HEAD_DIM + 1, :] for hh in range(ATT_HEADS)], axis=0)
    o_ref[...] = att_t.T.astype(o_ref.dtype)


def _dsa_call(aqp, iqt, gt, ak, ik, avt, n_sel, b, s, t):
    nq = s // t

    def qblk_t(r):
        return pl.BlockSpec((r, t), lambda bi, qi: (0, bi * nq + qi))

    def kblk(w):
        return pl.BlockSpec((None, s, w), lambda bi, qi: (bi, 0, 0))

    return pl.pallas_call(
        functools.partial(_dsa_kernel, n_sel=n_sel, t=t),
        grid=(b, nq),
        in_specs=[qblk_t(ATT_HEADS * Q_PAD), qblk_t(IDX_W), qblk_t(GT_ROWS),
                  pl.BlockSpec((GT_ROWS, s), lambda bi, qi: (0, bi)), kblk(KV_W), kblk(LANES),
                  pl.BlockSpec((nq, ATT_KV_HEADS * V_AUG, t), lambda bi, qi: (bi, 0, 0))],
        out_specs=pl.BlockSpec((None, t, ATT_W), lambda bi, qi: (bi, qi, 0)),
        out_shape=jax.ShapeDtypeStruct((b, s, ATT_W), bf16),
        scratch_shapes=[
            pltpu.VMEM((nq, t, t), f32),
            pltpu.VMEM((KEY_BITS, nq, SUBLANES, t), i32),
            pltpu.VMEM((ATT_HEADS, nq, t, t), f32),
            pltpu.VMEM((1, t), i32),
            pltpu.VMEM((ATT_HEADS, t), f32),
            pltpu.VMEM((ATT_HEADS, V_AUG, t), f32),
            pltpu.VMEM((V_AUG - HEAD_DIM, t), f32),
        ],
        compiler_params=pltpu.CompilerParams(dimension_semantics=("parallel", "arbitrary"),
                                             vmem_limit_bytes=VMEM_LIMIT),
        name="dsa",
    )(aqp, iqt, gt, gt, ak, ik, avt)


def _mlstm_kernel(mq_ref, mk_ref, mv_ref, mo_ref, gt_ref, gb_ref, cwq_ref, cwk_ref, cbq_ref, cbk_ref, gmh_ref,
                  o_ref, *, chunk):
    hd = pl.program_id(1)
    seq = mq_ref.shape[0]
    n_chunks = seq // chunk
    row = lax.broadcasted_iota(i32, (chunk, chunk), 0)
    col = lax.broadcasted_iota(i32, (chunk, chunk), 1)
    tril = col <= row
    eye = col == row
    row_t = lax.broadcasted_iota(i32, (SUBLANES, ML_DIM), 0)

    def conv_silu(x_ref, w_ref, b_ref, c):
        t0 = c * chunk
        cur = x_ref[t0:t0 + chunk, :]
        acc = cur * w_ref[CONV_W - 1:CONV_W, :] + b_ref[...]
        for j in range(1, CONV_W):
            if c > 0:
                shifted = x_ref[t0 - j:t0 - j + chunk, :]
            else:
                rolled = pltpu.roll(cur, j, 0)
                first = jnp.where(row_t >= j, rolled[0:SUBLANES], 0.0)
                shifted = jnp.concatenate([first, rolled[SUBLANES:]], axis=0)
            acc = acc + shifted * w_ref[CONV_W - 1 - j:CONV_W - j, :]
        return acc * _sigmoid(acc)

    c_state = jnp.zeros((ML_DIM, ML_DIM), f32)
    n_state = jnp.zeros((1, ML_DIM), f32)
    m_state = jnp.zeros((1, 1), f32)
    li_all = gt_ref[pl.ds(GT_MI + hd, 1), :] + gb_ref[pl.ds(hd, 1), :]
    f_all = gt_ref[pl.ds(GT_MF + hd, 1), :] + gb_ref[pl.ds(ML_HEADS + hd, 1), :]
    lf_all = -(jnp.maximum(-f_all, 0.0) + jnp.log1p(jnp.exp(-jnp.abs(f_all))))
    for c in range(n_chunks):
        t0 = c * chunk
        q = conv_silu(mq_ref, cwq_ref, cbq_ref, c) * (ML_DIM ** -0.5)
        k = conv_silu(mk_ref, cwk_ref, cbk_ref, c)
        qb = q.astype(bf16)
        kb = k.astype(bf16)
        vb = mv_ref[t0:t0 + chunk, :]
        li_row = li_all[:, t0:t0 + chunk]
        lf_row = lf_all[:, t0:t0 + chunk]
        b_col = jnp.sum(jnp.where(tril, lf_row, 0.0), axis=1, keepdims=True)
        b_row = jnp.sum(jnp.where(eye, b_col, 0.0), axis=0, keepdims=True)
        li_col = jnp.sum(jnp.where(eye, li_row, 0.0), axis=1, keepdims=True)
        b_last = jnp.sum(lf_row, axis=1, keepdims=True)

        log_d = jnp.where(tril, b_col - b_row + li_row, -jnp.inf)
        inter = b_col + m_state
        m_t = jnp.maximum(inter, jnp.max(log_d, axis=1, keepdims=True))
        dmat = jnp.exp(log_d - m_t)
        inter_w = jnp.exp(inter - m_t)
        qk = _dot_nt(qb, kb) * dmat
        num = inter_w * _dot(qb, c_state.astype(bf16)) + _dot(qk.astype(bf16), vb)
        den = inter_w * jnp.sum(q * n_state, axis=1, keepdims=True) + jnp.sum(qk, axis=1, keepdims=True)
        h_t = num / jnp.maximum(jnp.abs(den), jnp.exp(-m_t))

        log_g = b_last - b_col + li_col
        m_new = jnp.maximum(b_last + m_state, jnp.max(log_g, axis=0, keepdims=True))
        g = jnp.exp(log_g - m_new)
        decay = jnp.exp(b_last + m_state - m_new)
        gk = g * k
        c_state = decay * c_state + _dot(gk.T.astype(bf16), vb)
        n_state = decay * n_state + jnp.sum(gk, axis=0, keepdims=True)
        m_state = m_new

        hn = h_t * lax.rsqrt(jnp.mean(h_t * h_t, axis=1, keepdims=True) + EPS) * gmh_ref[...]
        o_ref[t0:t0 + chunk, :] = (_sigmoid(mo_ref[t0:t0 + chunk, :]) * hn).astype(o_ref.dtype)


def _mlstm_call(mq, mk, mv, mo, gt, gb, conv_w, conv_b, gmh):
    b, s, _ = mq.shape
    chunk = min(ML_CHUNK, s)

    def head_blk():
        return pl.BlockSpec((None, s, ML_DIM), lambda bi, hi: (bi, 0, hi))

    return pl.pallas_call(
        functools.partial(_mlstm_kernel, chunk=chunk),
        grid=(b, ML_HEADS),
        in_specs=[head_blk(), head_blk(), head_blk(), head_blk(),
                  pl.BlockSpec((GT_ROWS, s), lambda bi, hi: (0, bi)),
                  pl.BlockSpec((2 * ML_HEADS, 1), lambda bi, hi: (0, 0)),
                  pl.BlockSpec((CONV_W, ML_DIM), lambda bi, hi: (0, hi)),
                  pl.BlockSpec((CONV_W, ML_DIM), lambda bi, hi: (0, ML_HEADS + hi)),
                  pl.BlockSpec((1, ML_DIM), lambda bi, hi: (0, hi)),
                  pl.BlockSpec((1, ML_DIM), lambda bi, hi: (0, ML_HEADS + hi)),
                  pl.BlockSpec((None, 1, ML_DIM), lambda bi, hi: (hi, 0, 0))],
        out_specs=head_blk(),
        out_shape=jax.ShapeDtypeStruct((b, s, ML_W), bf16),
        compiler_params=pltpu.CompilerParams(dimension_semantics=("parallel", "parallel"),
                                             vmem_limit_bytes=VMEM_LIMIT),
        name="mlstm",
    )(mq, mk, mv, mo, gt, gb, conv_w, conv_w, conv_b, conv_b, gmh)


def _mix_kernel(h_ref, att_ref, hm_ref, p_ref, wo_ref, gmlp_ref, w1_ref, w2_ref, gple_ref, wg_ref, bg_ref, wp_ref,
                o_ref):
    def rms(v, g_ref):
        return (v * lax.rsqrt(jnp.mean(v * v, axis=-1, keepdims=True) + EPS) * g_ref[...]).astype(bf16)

    mixed = _dot(att_ref[...], wo_ref[0:ATT_W, :]) + _dot(hm_ref[...], wo_ref[ATT_W:ATT_W + ML_W, :])
    h1 = h_ref[...] + mixed
    xn = rms(h1, gmlp_ref)
    mlp = None
    for f in range(D_FF // FF_CHUNK):
        u = jnp.maximum(_dot(xn, w1_ref[:, f * FF_CHUNK:(f + 1) * FF_CHUNK]), 0.0)
        part = _dot((u * u).astype(bf16), w2_ref[f * FF_CHUNK:(f + 1) * FF_CHUNK, :])
        mlp = part if mlp is None else mlp + part
    h2 = h1 + mlp
    gate = _sigmoid(_dot(rms(h2, gple_ref), wg_ref[...]) + bg_ref[...])
    o_ref[...] = h2 + gate * _dot(p_ref[...].astype(bf16), wp_ref[...])


def _mix_call(h, att, hm, p, wo, gmlp, w1, w2, gple, wg, bg, wp):
    n = h.shape[0]
    tm = min(MIX_TM, n)

    def tok(w):
        return pl.BlockSpec((tm, w), lambda i: (i, 0))

    def full(a):
        return pl.BlockSpec(a.shape, lambda i: (0,) * a.ndim, pipeline_mode=pl.Buffered(1))

    return pl.pallas_call(
        _mix_kernel, grid=(n // tm,),
        in_specs=[tok(D_MODEL), tok(ATT_W), tok(ML_W), tok(PLE_DIM)]
                 + [full(a) for a in (wo, gmlp, w1, w2, gple, wg, bg, wp)],
        out_specs=tok(D_MODEL),
        out_shape=jax.ShapeDtypeStruct((n, D_MODEL), f32),
        compiler_params=pltpu.CompilerParams(dimension_semantics=("parallel",), vmem_limit_bytes=VMEM_LIMIT),
        name="mix",
    )(h, att, hm, p, wo, gmlp, w1, w2, gple, wg, bg, wp)


def kernel(x, p, positions, g_mix, w_in, g_cq, w_q_up, w_iq_up, g_qn, g_kn, g_ik, conv_w, conv_b, i_bias, f_bias,
           g_mh, w_out, g_mlp, w_ff1, w_ff2, g_ple, w_ple_gate, b_ple_gate, w_ple):
    b, s, d = x.shape
    n = b * s
    depth = p.shape[0]
    n_sel = min(TOPK_MAX, s // 4)
    t = min(DSA_T, s)
    h = x.reshape(n, d)
    pos = positions.reshape(1, n).astype(i32)
    invf = (ROPE_THETA ** (-(jnp.arange(ROT_HALF, dtype=f32) * 2.0) / ROT_DIM))[:, None]

    split = np.cumsum(IN_SIZES)[:-1].tolist()
    for i in range(depth):
        c_q, a_k, a_v, i_k, i_w, m_q, m_k, m_v, m_o, m_i, m_f = jnp.split(w_in[i], split, axis=1)
        wmain = jnp.concatenate([c_q, m_q, m_k, m_v, m_o], axis=1).astype(bf16)
        wt = jnp.concatenate([a_k, i_k, a_v, i_w, m_i, m_f], axis=1).T.astype(bf16)
        wqupt = jnp.concatenate([w_q_up[i], w_iq_up[i]], axis=1).T.astype(bf16)
        aqt, iqt, ak, ik, avt, gt, mq, mk, mv, mo = _proj_call(
            h, pos, g_mix[i][None, :], wmain, wt, g_cq[i][None, :], wqupt,
            g_qn[i][:, None], g_kn[i][:, None], g_ik[i][:, None], invf, t)

        def b3(a):
            return a.reshape(b, s, a.shape[-1])

        att = _dsa_call(aqt, iqt, gt, b3(ak), b3(ik), avt, n_sel, b, s, t)
        gb = jnp.concatenate([i_bias[i], f_bias[i]])[:, None]
        hm = _mlstm_call(b3(mq), b3(mk), b3(mv), b3(mo), gt, gb, conv_w[i], conv_b[i][None, :],
                         g_mh[i][:, None, :])
        h = _mix_call(h, att.reshape(n, ATT_W), hm.reshape(n, ML_W), p[i].reshape(n, PLE_DIM),
                      w_out[i].astype(bf16), g_mlp[i][None, :], w_ff1[i].astype(bf16), w_ff2[i].astype(bf16),
                      g_ple[i][None, :], w_ple_gate[i].astype(bf16), b_ple_gate[i][None, :], w_ple[i].astype(bf16))
    return h.reshape(b, s, d)
```

```python
import functools

import numpy as np
import jax
import jax.numpy as jnp
from jax import lax
from jax.experimental import pallas as pl
from jax.experimental.pallas import tpu as pltpu

D_MODEL = 1024
PLE_DIM = 256
ATT_HEADS = 8
ATT_KV_HEADS = 2
HEAD_DIM = 64
Q_RANK = 256
IDX_HEADS = 8
IDX_DIM = 64
TOPK_MAX = 256
ML_HEADS = 4
ML_DIM = 128
CONV_W = 4
D_FF = 4 * D_MODEL
ROPE_THETA = 500000.0
ROT_DIM = HEAD_DIM // 4
ROT_HALF = ROT_DIM // 2
EPS = 1e-6

ATT_W = ATT_HEADS * HEAD_DIM
KV_W = ATT_KV_HEADS * HEAD_DIM
ML_W = ML_HEADS * ML_DIM
IDX_W = IDX_HEADS * IDX_DIM
IN_SIZES = (Q_RANK, KV_W, KV_W, IDX_DIM, IDX_HEADS, ML_W, ML_W, ML_W, ML_W, ML_HEADS, ML_HEADS)
IDX_SCALE = (IDX_HEADS ** -0.5) * (IDX_DIM ** -0.5)
ATT_SCALE = HEAD_DIM ** -0.5
GQA = ATT_HEADS // ATT_KV_HEADS

LANES = 128
SUBLANES = 8
OFF_CQ = 0
OFF_MQ = Q_RANK
OFF_MK, OFF_MV, OFF_MO = OFF_MQ + ML_W, OFF_MQ + 2 * ML_W, OFF_MQ + 3 * ML_W
MAIN_W = OFF_MQ + 4 * ML_W
ROW_AK, ROW_IK, ROW_AV = 0, KV_W, KV_W + IDX_DIM
ROW_GT = ROW_AV + KV_W
GT_IW, GT_MI, GT_MF = 0, IDX_HEADS, IDX_HEADS + ML_HEADS
GT_PROJ = IDX_HEADS + 2 * ML_HEADS
GT_QN2 = GT_PROJ
GT_KN2 = GT_QN2 + ATT_HEADS
GT_ROWS = 32
T_ROWS = ROW_GT + GT_PROJ
Q_PAD = 2 * HEAD_DIM
V_AUG = HEAD_DIM + 16
KEY_BITS = 32
CHUNK_UNROLL = 4
LOG2E = 1.4426950408889634
BOUND_SLACK = 1.001
BOUND_LIMIT = 40.0

PROJ_TM = 512
MIX_TM = 512
FF_CHUNK = 1024
DSA_T = 256
ML_CHUNK = 256
NEG_BIG = -1e30
VMEM_LIMIT = 56 * 1024 * 1024

_NT = (((1,), (1,)), ((), ()))

f32 = jnp.float32
bf16 = jnp.bfloat16
i32 = jnp.int32
INT_MIN = -2 ** 31


def _dot(a, b):
    return jnp.dot(a, b, preferred_element_type=f32)


def _dot_nt(a, b):
    return lax.dot_general(a, b, _NT, preferred_element_type=f32)


def _sigmoid(x):
    return 0.5 * jnp.tanh(0.5 * x) + 0.5


def _bit_transpose32(words):
    a = list(words)
    j, m = 16, 0x0000FFFF
    while j:
        mask = int(np.array(m, np.uint32).view(np.int32))
        k = 0
        while k < 32:
            tmp = (a[k] ^ lax.shift_right_logical(a[k + j], jnp.int32(j))) & mask
            a[k] = a[k] ^ tmp
            a[k + j] = a[k + j] ^ lax.shift_left(tmp, jnp.int32(j))
            k = (k + j + 1) & ~j
        j >>= 1
        m = (m ^ (m << j)) & 0xFFFFFFFF
    return a


def _fold_rows(x, op):
    x = x.reshape(x.shape[0] // SUBLANES, SUBLANES, x.shape[1])
    while x.shape[0] > 1:
        half = x.shape[0] // 2
        x = op(x[:half], x[half:])
    return x[0]


def _proj_kernel(h_ref, pos_ref, gmix_ref, wmain_ref, wt_ref, gcq_ref, wqupt_ref, gq_ref, gk_ref, gik_ref, invf_ref,
                 aqp_ref, iqt_ref, ak_ref, ik_ref, avt_ref, gt_ref, mq_ref, mk_ref, mv_ref, mo_ref):
    x = h_ref[...]
    ms = jnp.mean(x * x, axis=-1, keepdims=True)
    xn = (x * lax.rsqrt(ms + EPS) * gmix_ref[...]).astype(bf16)
    proj = _dot(xn, wmain_ref[...])
    pt = _dot_nt(wt_ref[...], xn)

    ang = invf_ref[...] * pos_ref[...].astype(f32)
    cos_t = jnp.cos(ang)
    sin_t = jnp.sin(ang)

    def rope_t(blk):
        x1 = blk[0:ROT_HALF]
        x2 = blk[ROT_HALF:ROT_DIM]
        return jnp.concatenate([x1 * cos_t - x2 * sin_t, x2 * cos_t + x1 * sin_t, blk[ROT_DIM:]], axis=0)

    def norm_t(blk, g_ref):
        return blk * lax.rsqrt(jnp.mean(blk * blk, axis=0, keepdims=True) + EPS) * g_ref[...]

    def head(a, j):
        return a[j * HEAD_DIM:(j + 1) * HEAD_DIM]

    cq = proj[:, OFF_CQ:OFF_CQ + Q_RANK]
    cqn = (cq * lax.rsqrt(jnp.mean(cq * cq, axis=-1, keepdims=True) + EPS) * gcq_ref[...]).astype(bf16)
    qqt = _dot_nt(wqupt_ref[...], cqn)
    tm = x.shape[0]

    def sq_norm_rows(blk_bf16):
        v = blk_bf16.astype(f32)
        return jnp.sum(v * v, axis=0, keepdims=True)

    zero_h = jnp.zeros((HEAD_DIM, tm), bf16)
    q_blocks, qn2 = [], []
    for j in range(ATT_HEADS):
        qb = (rope_t(norm_t(head(qqt, j), gq_ref)) * (ATT_SCALE * LOG2E)).astype(bf16)
        qn2.append(sq_norm_rows(qb))
        q_blocks += [qb if g == j // GQA else zero_h for g in range(ATT_KV_HEADS)]
    aqp_ref[...] = jnp.concatenate(q_blocks, axis=0)
    iqt = jnp.concatenate([rope_t(head(qqt, ATT_HEADS + j)) for j in range(IDX_HEADS)], axis=0)
    iqt_ref[...] = iqt.astype(bf16)

    akt = jnp.concatenate([rope_t(norm_t(head(pt, j), gk_ref)) for j in range(ATT_KV_HEADS)], axis=0)
    akb = akt.astype(bf16)
    kn2 = [sq_norm_rows(head(akb, g)) for g in range(ATT_KV_HEADS)]
    ak_ref[...] = akt.T.astype(bf16)
    ikt = rope_t(norm_t(pt[ROW_IK:ROW_IK + IDX_DIM], gik_ref))
    ikt = jnp.concatenate([ikt, jnp.zeros((LANES - IDX_DIM, tm), f32)], axis=0)
    ik_ref[...] = ikt.T.astype(bf16)
    avt = pt[ROW_AV:ROW_AV + KV_W].astype(bf16)
    ones = jnp.ones((V_AUG - HEAD_DIM, tm), bf16)
    avaug = jnp.concatenate([blk for g in range(ATT_KV_HEADS) for blk in (head(avt, g), ones)], axis=0)
    tk = avt_ref.shape[2]
    for j in range(avt_ref.shape[0]):
        avt_ref[j] = avaug[:, j * tk:(j + 1) * tk]
    gt_ref[...] = jnp.concatenate(
        [pt[ROW_GT:ROW_GT + GT_PROJ]] + qn2 + kn2
        + [jnp.zeros((GT_ROWS - GT_KN2 - ATT_KV_HEADS, tm), f32)], axis=0)

    mq_ref[...] = proj[:, OFF_MQ:OFF_MQ + ML_W]
    mk_ref[...] = proj[:, OFF_MK:OFF_MK + ML_W]
    mv_ref[...] = proj[:, OFF_MV:OFF_MV + ML_W].astype(bf16)
    mo_ref[...] = proj[:, OFF_MO:OFF_MO + ML_W]


def _proj_call(h, pos, gmix, wmain, wt, gcq, wqupt, gq, gk, gik, invf, tk):
    n = h.shape[0]
    tm = min(PROJ_TM, n)
    grid = (n // tm,)

    def tok(w):
        return pl.BlockSpec((tm, w), lambda i: (i, 0))

    def tok_t(r):
        return pl.BlockSpec((r, tm), lambda i: (0, i))

    def full(a):
        return pl.BlockSpec(a.shape, lambda i: (0,) * a.ndim)

    out_shape = [
        jax.ShapeDtypeStruct((ATT_HEADS * Q_PAD, n), bf16),
        jax.ShapeDtypeStruct((IDX_W, n), bf16),
        jax.ShapeDtypeStruct((n, KV_W), bf16),
        jax.ShapeDtypeStruct((n, LANES), bf16),
        jax.ShapeDtypeStruct((n // tk, ATT_KV_HEADS * V_AUG, tk), bf16),
        jax.ShapeDtypeStruct((GT_ROWS, n), f32),
        jax.ShapeDtypeStruct((n, ML_W), f32),
        jax.ShapeDtypeStruct((n, ML_W), f32),
        jax.ShapeDtypeStruct((n, ML_W), bf16),
        jax.ShapeDtypeStruct((n, ML_W), f32),
    ]
    out_specs = [tok_t(ATT_HEADS * Q_PAD), tok_t(IDX_W), tok(KV_W), tok(LANES),
                 pl.BlockSpec((tm // tk, ATT_KV_HEADS * V_AUG, tk), lambda i: (i, 0, 0)), tok_t(GT_ROWS),
                 tok(ML_W), tok(ML_W), tok(ML_W), tok(ML_W)]
    in_specs = [tok(D_MODEL), tok_t(1)] + [full(a) for a in (gmix, wmain, wt, gcq, wqupt, gq, gk, gik, invf)]
    return pl.pallas_call(
        _proj_kernel, grid=grid, in_specs=in_specs, out_specs=out_specs, out_shape=out_shape,
        compiler_params=pltpu.CompilerParams(dimension_semantics=("parallel",), vmem_limit_bytes=VMEM_LIMIT),
        name="proj",
    )(h, pos, gmix, wmain, wt, gcq, wqupt, gq, gk, gik, invf)


def _dsa_kernel(aqp_ref, iqt_ref, gt_ref, gtk_ref, ak_ref, ik_ref, avt_ref, o_ref,
                sc_ref, plane_ref, s_ref, cut_ref, m_ref, acc_ref, kept_ref, *, n_sel, t):
    qi = pl.program_id(1)
    nk = qi + 1
    k_loc = lax.broadcasted_iota(i32, (t, t), 0)
    q_pos = lax.broadcasted_iota(i32, (t, t), 1) + qi * t
    w_idx = gt_ref[GT_IW:GT_IW + IDX_HEADS, :] * IDX_SCALE

    def for_each_chunk(chunk_work):
        def group(i, carry):
            for j in range(CHUNK_UNROLL):
                chunk_work(CHUNK_UNROLL * i + j)
            return carry
        lax.fori_loop(0, nk // CHUNK_UNROLL, group, 0)
        rem = nk % CHUNK_UNROLL

        @pl.when(rem >= 2)
        def _():
            chunk_work(nk - rem)
            chunk_work(nk - rem + 1)

        @pl.when(rem % 2 == 1)
        def _():
            chunk_work(nk - 1)

    k_max2 = jnp.max(gtk_ref[GT_KN2:GT_KN2 + ATT_KV_HEADS, :], axis=1, keepdims=True)
    bound = jnp.concatenate(
        [jnp.sqrt(gt_ref[GT_QN2 + hh:GT_QN2 + hh + 1, :] * k_max2[hh // GQA:hh // GQA + 1, :])
         for hh in range(ATT_HEADS)], axis=0) * BOUND_SLACK + (BOUND_SLACK - 1.0)

    @pl.when(qi == 0)
    def _():
        plane_ref[...] = jnp.zeros(plane_ref.shape, i32)

    def score_chunk(kc):
        off = pl.multiple_of(kc * t, t)
        ikc = ik_ref[pl.ds(off, t), 0:IDX_DIM]
        s = jnp.zeros((t, t), f32)
        for hh in range(IDX_HEADS):
            lg = _dot(ikc, iqt_ref[hh * IDX_DIM:(hh + 1) * IDX_DIM, :])
            s = s + w_idx[hh:hh + 1, :] * jnp.maximum(lg, 0.0)
        causal = k_loc + kc * t <= q_pos
        sc_ref[kc] = jnp.where(causal, s, -jnp.inf)
        bits = pltpu.bitcast(s, i32)
        key = bits ^ ((bits >> 31) & 0x7FFFFFFF)
        key = jnp.where(bits == INT_MIN, 0, key)
        key = jnp.where(causal, key, INT_MIN)
        u = key ^ INT_MIN
        planes = _bit_transpose32([u[SUBLANES * i:SUBLANES * (i + 1), :] for i in range(KEY_BITS)])
        for b in range(KEY_BITS):
            plane_ref[b, kc] = planes[b]
        rows = ak_ref[pl.ds(off, t), :]
        for hh in range(ATT_HEADS):
            s_ref[hh, kc] = _dot(rows, aqp_ref[hh * Q_PAD:(hh + 1) * Q_PAD, :])

    for_each_chunk(score_chunk)

    def popcount_rows(words):
        pc = lax.population_count(words)
        return jnp.sum(jnp.sum(pc, axis=0).astype(f32), axis=0, keepdims=True)

    def bit_body(it, carry):
        t_u, above, eq = carry
        ones = eq & plane_ref[it]
        c1 = popcount_rows(ones)
        ok = above + c1 >= n_sel
        t_u = jnp.where(ok, t_u | lax.shift_left(jnp.int32(1), KEY_BITS - 1 - it), t_u)
        return t_u, jnp.where(ok, above, above + c1), jnp.where(ok, ones, eq ^ ones)

    n_chunks = plane_ref.shape[1]
    word_chunk = lax.broadcasted_iota(i32, (n_chunks, SUBLANES, t), 0)
    eq0 = jnp.where(word_chunk < nk, -1, 0)
    t_u, above, eq = lax.fori_loop(0, KEY_BITS, bit_body,
                                   (jnp.zeros((1, t), i32), jnp.zeros((1, t), f32), eq0))
    cnt_t = above + popcount_rows(eq)
    t_s = t_u ^ INT_MIN

    cut_ref[...] = jnp.full((1, t), 2 ** 30, i32)

    @pl.when(jnp.max(cnt_t) > n_sel)
    def _():
        need = n_sel - above
        first_pos = word_chunk * t + lax.broadcasted_iota(i32, (n_chunks, SUBLANES, t), 1)

        def ties_below(bound_pos):
            n_top = jnp.clip((bound_pos - first_pos + (SUBLANES - 1)) >> 3, 0, KEY_BITS)
            top = lax.shift_right_arithmetic(jnp.full(n_top.shape, INT_MIN, i32), jnp.maximum(n_top - 1, 0))
            return popcount_rows(eq & jnp.where(n_top > 0, top, 0))

        c = jnp.zeros((1, t), i32)
        n_bits = max(1, int(ak_ref.shape[0] - 1).bit_length())
        for b in range(n_bits, -1, -1):
            cand = c | (1 << b)
            c = jnp.where(ties_below(cand) <= need, cand, c)
        cut_ref[...] = c

    def as_f32(m):
        return jnp.where(m, 1.0, 0.0)

    def selected(kc, thr, cut):
        sc = sc_ref[kc]
        idx = k_loc + kc * t
        return ((sc > thr) | ((sc == thr) & (idx < cut))) & (idx <= q_pos)

    def add_value_products(kc, hh, p_bf16):
        g = hh // GQA
        acc_ref[hh] += _dot(avt_ref[kc, g * V_AUG:(g + 1) * V_AUG, :], p_bf16)

    shift_is_safe = jnp.max(bound) <= BOUND_LIMIT
    ones_rows = jnp.ones((kept_ref.shape[0], t), bf16)

    def attention(thr, cut, count_kept):
        acc_ref[...] = jnp.zeros(acc_ref.shape, f32)
        if count_kept:
            kept_ref[...] = jnp.zeros(kept_ref.shape, f32)

        @pl.when(shift_is_safe)
        def _():
            def att_chunk(kc):
                keep = as_f32(selected(kc, thr, cut)).astype(bf16)
                if count_kept:
                    kept_ref[...] += _dot(ones_rows, keep)
                for hh in range(ATT_HEADS):
                    add_value_products(kc, hh, jnp.exp2(s_ref[hh, kc]).astype(bf16) * keep)
            for_each_chunk(att_chunk)

        @pl.when(jnp.logical_not(shift_is_safe))
        def _():
            def max_body(kc, mx):
                bias = jnp.where(selected(kc, thr, cut), 0.0, NEG_BIG)
                rows = [jnp.max(_fold_rows(s_ref[hh, kc] + bias, jnp.maximum), axis=0, keepdims=True)
                        for hh in range(ATT_HEADS)]
                return jnp.maximum(mx, jnp.concatenate(rows, axis=0))
            m_ref[...] = lax.fori_loop(0, nk, max_body, jnp.full((ATT_HEADS, t), NEG_BIG, f32))

            def att_body(kc, carry):
                sel = selected(kc, thr, cut)
                if count_kept:
                    kept_ref[...] += _dot(ones_rows, as_f32(sel).astype(bf16))
                bias = jnp.where(sel, 0.0, NEG_BIG)
                for hh in range(ATT_HEADS):
                    p = jnp.exp2(s_ref[hh, kc] + bias - m_ref[hh:hh + 1, :])
                    add_value_products(kc, hh, p.astype(bf16))
                return carry
            lax.fori_loop(0, nk, att_body, 0)

    tb = jnp.where(t_s < 0, t_s ^ 0x7FFFFFFF, t_s)
    thr0 = jnp.where(t_s == INT_MIN, -jnp.inf, pltpu.bitcast(tb, f32))
    cut0 = cut_ref[...]
    attention(thr0, cut0, True)
    n_causal = (lax.broadcasted_iota(i32, (1, t), 1) + (qi * t + 1)).astype(f32)
    kept_is_off = kept_ref[0:1, :] != jnp.minimum(n_causal, float(n_sel))

    @pl.when(jnp.max(as_f32(kept_is_off)) > 0.0)
    def _():
        def over_chunks(stat_fn, inits, ops):
            def body(kc, parts):
                vals = stat_fn(sc_ref[kc], k_loc + kc * t)
                return tuple(op(p, _fold_rows(v, op)) for p, v, op in zip(parts, vals, ops))
            parts = lax.fori_loop(0, nk, body, tuple(jnp.full((SUBLANES, t), v, f32) for v in inits))
            reducers = {jnp.add: jnp.sum, jnp.minimum: jnp.min, jnp.maximum: jnp.max}
            return tuple(reducers[op](p, axis=0, keepdims=True) for p, op in zip(parts, ops))

        def counts(thr):
            return over_chunks(
                lambda sc, idx: (as_f32(sc > thr), as_f32(sc == thr), as_f32((sc == thr) & (idx < cut0))),
                (0.0, 0.0, 0.0), (jnp.add, jnp.add, jnp.add))

        def polish_cond(state):
            _, n_above, n_equal, _ = state
            return jnp.max(as_f32((n_above >= n_sel) | (n_above + n_equal < n_sel))) > 0.0

        def polish_body(state):
            thr, n_above, n_equal, _ = state
            up, down = over_chunks(
                lambda sc, idx: (jnp.where(sc > thr, sc, jnp.inf), jnp.where(sc < thr, sc, -jnp.inf)),
                (jnp.inf, -jnp.inf), (jnp.minimum, jnp.maximum))
            thr = jnp.where(n_above >= n_sel, up, jnp.where(n_above + n_equal < n_sel, down, thr))
            return (thr,) + counts(thr)

        thr, n_above, n_equal, n_kept = lax.while_loop(polish_cond, polish_body, (thr0,) + counts(thr0))
        need = n_sel - n_above
        cut_is_off = n_kept != jnp.minimum(n_equal, need)

        @pl.when(jnp.max(as_f32(cut_is_off)) > 0.0)
        def _():
            c = jnp.zeros((1, t), i32)
            n_bits = max(1, int(ak_ref.shape[0] - 1).bit_length())
            for b in range(n_bits, -1, -1):
                cand = c | (1 << b)
                (below,) = over_chunks(lambda sc, idx: (as_f32((sc == thr) & (idx < cand)),), (0.0,), (jnp.add,))
                c = jnp.where(below <= need, cand, c)
            cut_ref[...] = c

        attention(thr, cut_ref[...], False)

    att_t = jnp.concatenate(
        [acc_ref[hh, 0:HEAD_DIM, :] / acc_ref[hh, HEAD_DIM:HEAD_DIM + 1, :] for hh in range(ATT_HEADS)], axis=0)
    o_ref[...] = att_t.T.astype(o_ref.dtype)


def _dsa_call(aqp, iqt, gt, ak, ik, avt, n_sel, b, s, t):
    nq = s // t

    def qblk_t(r):
        return pl.BlockSpec((r, t), lambda bi, qi: (0, bi * nq + qi))

    def kblk(w):
        return pl.BlockSpec((None, s, w), lambda bi, qi: (bi, 0, 0))

    return pl.pallas_call(
        functools.partial(_dsa_kernel, n_sel=n_sel, t=t),
        grid=(b, nq),
        in_specs=[qblk_t(ATT_HEADS * Q_PAD), qblk_t(IDX_W), qblk_t(GT_ROWS),
                  pl.BlockSpec((GT_ROWS, s), lambda bi, qi: (0, bi)), kblk(KV_W), kblk(LANES),
                  pl.BlockSpec((nq, ATT_KV_HEADS * V_AUG, t), lambda bi, qi: (bi, 0, 0))],
        out_specs=pl.BlockSpec((None, t, ATT_W), lambda bi, qi: (bi, qi, 0)),
        out_shape=jax.ShapeDtypeStruct((b, s, ATT_W), bf16),
        scratch_shapes=[
            pltpu.VMEM((nq, t, t), f32),
            pltpu.VMEM((KEY_BITS, nq, SUBLANES, t), i32),
            pltpu.VMEM((ATT_HEADS, nq, t, t), f32),
            pltpu.VMEM((1, t), i32),
            pltpu.VMEM((ATT_HEADS, t), f32),
            pltpu.VMEM((ATT_HEADS, V_AUG, t), f32),
            pltpu.VMEM((V_AUG - HEAD_DIM, t), f32),
        ],
        compiler_params=pltpu.CompilerParams(dimension_semantics=("parallel", "arbitrary"),
                                             vmem_limit_bytes=VMEM_LIMIT),
        name="dsa",
    )(aqp, iqt, gt, gt, ak, ik, avt)


def _mlstm_kernel(mq_ref, mk_ref, mv_ref, mo_ref, gt_ref, gb_ref, cwq_ref, cwk_ref, cbq_ref, cbk_ref, gmh_ref,
                  o_ref, *, chunk):
    hd = pl.program_id(1)
    seq = mq_ref.shape[0]
    n_chunks = seq // chunk
    row = lax.broadcasted_iota(i32, (chunk, chunk), 0)
    col = lax.broadcasted_iota(i32, (chunk, chunk), 1)
    tril = col <= row
    eye = col == row
    row_t = lax.broadcasted_iota(i32, (SUBLANES, ML_DIM), 0)

    def conv_silu(x_ref, w_ref, b_ref, c):
        t0 = c * chunk
        cur = x_ref[t0:t0 + chunk, :]
        acc = cur * w_ref[CONV_W - 1:CONV_W, :] + b_ref[...]
        for j in range(1, CONV_W):
            if c > 0:
                shifted = x_ref[t0 - j:t0 - j + chunk, :]
            else:
                rolled = pltpu.roll(cur, j, 0)
                first = jnp.where(row_t >= j, rolled[0:SUBLANES], 0.0)
                shifted = jnp.concatenate([first, rolled[SUBLANES:]], axis=0)
            acc = acc + shifted * w_ref[CONV_W - 1 - j:CONV_W - j, :]
        return acc * _sigmoid(acc)

    c_state = jnp.zeros((ML_DIM, ML_DIM), f32)
    n_state = jnp.zeros((1, ML_DIM), f32)
    m_state = jnp.zeros((1, 1), f32)
    li_all = gt_ref[pl.ds(GT_MI + hd, 1), :] + gb_ref[pl.ds(hd, 1), :]
    f_all = gt_ref[pl.ds(GT_MF + hd, 1), :] + gb_ref[pl.ds(ML_HEADS + hd, 1), :]
    lf_all = -(jnp.maximum(-f_all, 0.0) + jnp.log1p(jnp.exp(-jnp.abs(f_all))))
    for c in range(n_chunks):
        t0 = c * chunk
        q = conv_silu(mq_ref, cwq_ref, cbq_ref, c) * (ML_DIM ** -0.5)
        k = conv_silu(mk_ref, cwk_ref, cbk_ref, c)
        qb = q.astype(bf16)
        kb = k.astype(bf16)
        vb = mv_ref[t0:t0 + chunk, :]
        li_row = li_all[:, t0:t0 + chunk]
        lf_row = lf_all[:, t0:t0 + chunk]
        b_col = jnp.sum(jnp.where(tril, lf_row, 0.0), axis=1, keepdims=True)
        b_row = jnp.sum(jnp.where(eye, b_col, 0.0), axis=0, keepdims=True)
        li_col = jnp.sum(jnp.where(eye, li_row, 0.0), axis=1, keepdims=True)
        b_last = jnp.sum(lf_row, axis=1, keepdims=True)

        log_d = jnp.where(tril, b_col - b_row + li_row, -jnp.inf)
        inter = b_col + m_state
        m_t = jnp.maximum(inter, jnp.max(log_d, axis=1, keepdims=True))
        dmat = jnp.exp(log_d - m_t)
        inter_w = jnp.exp(inter - m_t)
        qk = _dot_nt(qb, kb) * dmat
        num = inter_w * _dot(qb, c_state.astype(bf16)) + _dot(qk.astype(bf16), vb)
        den = inter_w * jnp.sum(q * n_state, axis=1, keepdims=True) + jnp.sum(qk, axis=1, keepdims=True)
        h_t = num / jnp.maximum(jnp.abs(den), jnp.exp(-m_t))

        log_g = b_last - b_col + li_col
        m_new = jnp.maximum(b_last + m_state, jnp.max(log_g, axis=0, keepdims=True))
        g = jnp.exp(log_g - m_new)
        decay = jnp.exp(b_last + m_state - m_new)
        gk = g * k
        c_state = decay * c_state + _dot(gk.T.astype(bf16), vb)
        n_state = decay * n_state + jnp.sum(gk, axis=0, keepdims=True)
        m_state = m_new

        hn = h_t * lax.rsqrt(jnp.mean(h_t * h_t, axis=1, keepdims=True) + EPS) * gmh_ref[...]
        o_ref[t0:t0 + chunk, :] = (_sigmoid(mo_ref[t0:t0 + chunk, :]) * hn).astype(o_ref.dtype)


def _mlstm_call(mq, mk, mv, mo, gt, gb, conv_w, conv_b, gmh):
    b, s, _ = mq.shape
    chunk = min(ML_CHUNK, s)

    def head_blk():
        return pl.BlockSpec((None, s, ML_DIM), lambda bi, hi: (bi, 0, hi))

    return pl.pallas_call(
        functools.partial(_mlstm_kernel, chunk=chunk),
        grid=(b, ML_HEADS),
        in_specs=[head_blk(), head_blk(), head_blk(), head_blk(),
                  pl.BlockSpec((GT_ROWS, s), lambda bi, hi: (0, bi)),
                  pl.BlockSpec((2 * ML_HEADS, 1), lambda bi, hi: (0, 0)),
                  pl.BlockSpec((CONV_W, ML_DIM), lambda bi, hi: (0, hi)),
                  pl.BlockSpec((CONV_W, ML_DIM), lambda bi, hi: (0, ML_HEADS + hi)),
                  pl.BlockSpec((1, ML_DIM), lambda bi, hi: (0, hi)),
                  pl.BlockSpec((1, ML_DIM), lambda bi, hi: (0, ML_HEADS + hi)),
                  pl.BlockSpec((None, 1, ML_DIM), lambda bi, hi: (hi, 0, 0))],
        out_specs=head_blk(),
        out_shape=jax.ShapeDtypeStruct((b, s, ML_W), bf16),
        compiler_params=pltpu.CompilerParams(dimension_semantics=("parallel", "parallel"),
                                             vmem_limit_bytes=VMEM_LIMIT),
        name="mlstm",
    )(mq, mk, mv, mo, gt, gb, conv_w, conv_w, conv_b, conv_b, gmh)


def _mix_kernel(h_ref, att_ref, hm_ref, p_ref, wo_ref, gmlp_ref, w1_ref, w2_ref, gple_ref, wg_ref, bg_ref, wp_ref,
                o_ref):
    def rms(v, g_ref):
        return (v * lax.rsqrt(jnp.mean(v * v, axis=-1, keepdims=True) + EPS) * g_ref[...]).astype(bf16)

    mixed = _dot(att_ref[...], wo_ref[0:ATT_W, :]) + _dot(hm_ref[...], wo_ref[ATT_W:ATT_W + ML_W, :])
    h1 = h_ref[...] + mixed
    xn = rms(h1, gmlp_ref)
    mlp = None
    for f in range(D_FF // FF_CHUNK):
        u = jnp.maximum(_dot(xn, w1_ref[:, f * FF_CHUNK:(f + 1) * FF_CHUNK]), 0.0)
        part = _dot((u * u).astype(bf16), w2_ref[f * FF_CHUNK:(f + 1) * FF_CHUNK, :])
        mlp = part if mlp is None else mlp + part
    h2 = h1 + mlp
    gate = _sigmoid(_dot(rms(h2, gple_ref), wg_ref[...]) + bg_ref[...])
    o_ref[...] = h2 + gate * _dot(p_ref[...].astype(bf16), wp_ref[...])


def _mix_call(h, att, hm, p, wo, gmlp, w1, w2, gple, wg, bg, wp):
    n = h.shape[0]
    tm = min(MIX_TM, n)

    def tok(w):
        return pl.BlockSpec((tm, w), lambda i: (i, 0))

    def full(a):
        return pl.BlockSpec(a.shape, lambda i: (0,) * a.ndim, pipeline_mode=pl.Buffered(1))

    return pl.pallas_call(
        _mix_kernel, grid=(n // tm,),
        in_specs=[tok(D_MODEL), tok(ATT_W), tok(ML_W), tok(PLE_DIM)]
                 + [full(a) for a in (wo, gmlp, w1, w2, gple, wg, bg, wp)],
        out_specs=tok(D_MODEL),
        out_shape=jax.ShapeDtypeStruct((n, D_MODEL), f32),
        compiler_params=pltpu.CompilerParams(dimension_semantics=("parallel",), vmem_limit_bytes=VMEM_LIMIT),
        name="mix",
    )(h, att, hm, p, wo, gmlp, w1, w2, gple, wg, bg, wp)


def kernel(x, p, positions, g_mix, w_in, g_cq, w_q_up, w_iq_up, g_qn, g_kn, g_ik, conv_w, conv_b, i_bias, f_bias,
           g_mh, w_out, g_mlp, w_ff1, w_ff2, g_ple, w_ple_gate, b_ple_gate, w_ple):
    b, s, d = x.shape
    n = b * s
    depth = p.shape[0]
    n_sel = min(TOPK_MAX, s // 4)
    t = min(DSA_T, s)
    h = x.reshape(n, d)
    pos = positions.reshape(1, n).astype(i32)
    invf = (ROPE_THETA ** (-(jnp.arange(ROT_HALF, dtype=f32) * 2.0) / ROT_DIM))[:, None]

    split = np.cumsum(IN_SIZES)[:-1].tolist()
    for i in range(depth):
        c_q, a_k, a_v, i_k, i_w, m_q, m_k, m_v, m_o, m_i, m_f = jnp.split(w_in[i], split, axis=1)
        wmain = jnp.concatenate([c_q, m_q, m_k, m_v, m_o], axis=1).astype(bf16)
        wt = jnp.concatenate([a_k, i_k, a_v, i_w, m_i, m_f], axis=1).T.astype(bf16)
        wqupt = jnp.concatenate([w_q_up[i], w_iq_up[i]], axis=1).T.astype(bf16)
        aqt, iqt, ak, ik, avt, gt, mq, mk, mv, mo = _proj_call(
            h, pos, g_mix[i][None, :], wmain, wt, g_cq[i][None, :], wqupt,
            g_qn[i][:, None], g_kn[i][:, None], g_ik[i][:, None], invf, t)

        def b3(a):
            return a.reshape(b, s, a.shape[-1])

        att = _dsa_call(aqt, iqt, gt, b3(ak), b3(ik), avt, n_sel, b, s, t)
        gb = jnp.concatenate([i_bias[i], f_bias[i]])[:, None]
        hm = _mlstm_call(b3(mq), b3(mk), b3(mv), b3(mo), gt, gb, conv_w[i], conv_b[i][None, :],
                         g_mh[i][:, None, :])
        h = _mix_call(h, att.reshape(n, ATT_W), hm.reshape(n, ML_W), p[i].reshape(n, PLE_DIM),
                      w_out[i].astype(bf16), g_mlp[i][None, :], w_ff1[i].astype(bf16), w_ff2[i].astype(bf16),
                      g_ple[i][None, :], w_ple_gate[i].astype(bf16), b_ple_gate[i][None, :], w_ple[i].astype(bf16))
    return h.reshape(b, s, d)
```

```python
import functools

import numpy as np
import jax
import jax.numpy as jnp
from jax import lax
from jax.experimental import pallas as pl
from jax.experimental.pallas import tpu as pltpu

D_MODEL = 1024
PLE_DIM = 256
ATT_HEADS = 8
ATT_KV_HEADS = 2
HEAD_DIM = 64
Q_RANK = 256
IDX_HEADS = 8
IDX_DIM = 64
TOPK_MAX = 256
ML_HEADS = 4
ML_DIM = 128
CONV_W = 4
D_FF = 4 * D_MODEL
ROPE_THETA = 500000.0
ROT_DIM = HEAD_DIM // 4
ROT_HALF = ROT_DIM // 2
EPS = 1e-6

ATT_W = ATT_HEADS * HEAD_DIM
KV_W = ATT_KV_HEADS * HEAD_DIM
ML_W = ML_HEADS * ML_DIM
IDX_W = IDX_HEADS * IDX_DIM
IN_SIZES = (Q_RANK, KV_W, KV_W, IDX_DIM, IDX_HEADS, ML_W, ML_W, ML_W, ML_W, ML_HEADS, ML_HEADS)
IDX_SCALE = (IDX_HEADS ** -0.5) * (IDX_DIM ** -0.5)
ATT_SCALE = HEAD_DIM ** -0.5
GQA = ATT_HEADS // ATT_KV_HEADS

LANES = 128
SUBLANES = 8
OFF_CQ = 0
OFF_MQ = Q_RANK
OFF_MK, OFF_MV, OFF_MO = OFF_MQ + ML_W, OFF_MQ + 2 * ML_W, OFF_MQ + 3 * ML_W
MAIN_W = OFF_MQ + 4 * ML_W
ROW_AK, ROW_IK, ROW_AV = 0, KV_W, KV_W + IDX_DIM
ROW_GT = ROW_AV + KV_W
GT_IW, GT_MI, GT_MF = 0, IDX_HEADS, IDX_HEADS + ML_HEADS
GT_PROJ = IDX_HEADS + 2 * ML_HEADS
GT_QN2 = GT_PROJ
GT_KN2 = GT_QN2 + ATT_HEADS
GT_ROWS = 32
T_ROWS = ROW_GT + GT_PROJ
Q_PAD = 2 * HEAD_DIM
V_AUG = HEAD_DIM + 16
KEY_BITS = 32
CHUNK_UNROLL = 4
LOG2E = 1.4426950408889634
BOUND_SLACK = 1.001
BOUND_LIMIT = 40.0

PROJ_TM = 512
MIX_TM = 512
FF_CHUNK = 1024
DSA_T = 256
ML_CHUNK = 256
NEG_BIG = -1e30
VMEM_LIMIT = 56 * 1024 * 1024

_NT = (((1,), (1,)), ((), ()))

f32 = jnp.float32
bf16 = jnp.bfloat16
i32 = jnp.int32
INT_MIN = -2 ** 31


def _dot(a, b):
    return jnp.dot(a, b, preferred_element_type=f32)


def _dot_nt(a, b):
    return lax.dot_general(a, b, _NT, preferred_element_type=f32)


def _sigmoid(x):
    return 0.5 * jnp.tanh(0.5 * x) + 0.5


def _bit_transpose32(words):
    a = list(words)
    j, m = 16, 0x0000FFFF
    while j:
        mask = int(np.array(m, np.uint32).view(np.int32))
        k = 0
        while k < 32:
            tmp = (a[k] ^ lax.shift_right_logical(a[k + j], jnp.int32(j))) & mask
            a[k] = a[k] ^ tmp
            a[k + j] = a[k + j] ^ lax.shift_left(tmp, jnp.int32(j))
            k = (k + j + 1) & ~j
        j >>= 1
        m = (m ^ (m << j)) & 0xFFFFFFFF
    return a


def _fold_rows(x, op):
    x = x.reshape(x.shape[0] // SUBLANES, SUBLANES, x.shape[1])
    while x.shape[0] > 1:
        half = x.shape[0] // 2
        x = op(x[:half], x[half:])
    return x[0]


def _proj_kernel(h_ref, pos_ref, gmix_ref, wmain_ref, wt_ref, gcq_ref, wqupt_ref, gq_ref, gk_ref, gik_ref, invf_ref,
                 aqp_ref, iqt_ref, ak_ref, ik_ref, avt_ref, gt_ref, mq_ref, mk_ref, mv_ref, mo_ref):
    x = h_ref[...]
    ms = jnp.mean(x * x, axis=-1, keepdims=True)
    xn = (x * lax.rsqrt(ms + EPS) * gmix_ref[...]).astype(bf16)
    proj = _dot(xn, wmain_ref[...])
    pt = _dot_nt(wt_ref[...], xn)

    ang = invf_ref[...] * pos_ref[...].astype(f32)
    cos_t = jnp.cos(ang)
    sin_t = jnp.sin(ang)

    def rope_t(blk):
        x1 = blk[0:ROT_HALF]
        x2 = blk[ROT_HALF:ROT_DIM]
        return jnp.concatenate([x1 * cos_t - x2 * sin_t, x2 * cos_t + x1 * sin_t, blk[ROT_DIM:]], axis=0)

    def norm_t(blk, g_ref):
        return blk * lax.rsqrt(jnp.mean(blk * blk, axis=0, keepdims=True) + EPS) * g_ref[...]

    def head(a, j):
        return a[j * HEAD_DIM:(j + 1) * HEAD_DIM]

    cq = proj[:, OFF_CQ:OFF_CQ + Q_RANK]
    cqn = (cq * lax.rsqrt(jnp.mean(cq * cq, axis=-1, keepdims=True) + EPS) * gcq_ref[...]).astype(bf16)
    qqt = _dot_nt(wqupt_ref[...], cqn)
    tm = x.shape[0]

    def sq_norm_rows(blk_bf16):
        v = blk_bf16.astype(f32)
        return jnp.sum(v * v, axis=0, keepdims=True)

    zero_h = jnp.zeros((HEAD_DIM, tm), bf16)
    q_blocks, qn2 = [], []
    for j in range(ATT_HEADS):
        qb = (rope_t(norm_t(head(qqt, j), gq_ref)) * (ATT_SCALE * LOG2E)).astype(bf16)
        qn2.append(sq_norm_rows(qb))
        q_blocks += [qb if g == j // GQA else zero_h for g in range(ATT_KV_HEADS)]
    aqp_ref[...] = jnp.concatenate(q_blocks, axis=0)
    iqt = jnp.concatenate([rope_t(head(qqt, ATT_HEADS + j)) for j in range(IDX_HEADS)], axis=0)
    iqt_ref[...] = iqt.astype(bf16)

    akt = jnp.concatenate([rope_t(norm_t(head(pt, j), gk_ref)) for j in range(ATT_KV_HEADS)], axis=0)
    akb = akt.astype(bf16)
    kn2 = [sq_norm_rows(head(akb, g)) for g in range(ATT_KV_HEADS)]
    ak_ref[...] = akt.T.astype(bf16)
    ikt = rope_t(norm_t(pt[ROW_IK:ROW_IK + IDX_DIM], gik_ref))
    ikt = jnp.concatenate([ikt, jnp.zeros((LANES - IDX_DIM, tm), f32)], axis=0)
    ik_ref[...] = ikt.T.astype(bf16)
    avt = pt[ROW_AV:ROW_AV + KV_W].astype(bf16)
    ones = jnp.ones((V_AUG - HEAD_DIM, tm), bf16)
    avaug = jnp.concatenate([blk for g in range(ATT_KV_HEADS) for blk in (head(avt, g), ones)], axis=0)
    tk = avt_ref.shape[2]
    for j in range(avt_ref.shape[0]):
        avt_ref[j] = avaug[:, j * tk:(j + 1) * tk]
    gt_ref[...] = jnp.concatenate(
        [pt[ROW_GT:ROW_GT + GT_PROJ]] + qn2 + kn2
        + [jnp.zeros((GT_ROWS - GT_KN2 - ATT_KV_HEADS, tm), f32)], axis=0)

    mq_ref[...] = proj[:, OFF_MQ:OFF_MQ + ML_W]
    mk_ref[...] = proj[:, OFF_MK:OFF_MK + ML_W]
    mv_ref[...] = proj[:, OFF_MV:OFF_MV + ML_W].astype(bf16)
    mo_ref[...] = proj[:, OFF_MO:OFF_MO + ML_W]


def _proj_call(h, pos, gmix, wmain, wt, gcq, wqupt, gq, gk, gik, invf, tk):
    n = h.shape[0]
    tm = min(PROJ_TM, n)
    grid = (n // tm,)

    def tok(w):
        return pl.BlockSpec((tm, w), lambda i: (i, 0))

    def tok_t(r):
        return pl.BlockSpec((r, tm), lambda i: (0, i))

    def full(a):
        return pl.BlockSpec(a.shape, lambda i: (0,) * a.ndim)

    out_shape = [
        jax.ShapeDtypeStruct((ATT_HEADS * Q_PAD, n), bf16),
        jax.ShapeDtypeStruct((IDX_W, n), bf16),
        jax.ShapeDtypeStruct((n, KV_W), bf16),
        jax.ShapeDtypeStruct((n, LANES), bf16),
        jax.ShapeDtypeStruct((n // tk, ATT_KV_HEADS * V_AUG, tk), bf16),
        jax.ShapeDtypeStruct((GT_ROWS, n), f32),
        jax.ShapeDtypeStruct((n, ML_W), f32),
        jax.ShapeDtypeStruct((n, ML_W), f32),
        jax.ShapeDtypeStruct((n, ML_W), bf16),
        jax.ShapeDtypeStruct((n, ML_W), f32),
    ]
    out_specs = [tok_t(ATT_HEADS * Q_PAD), tok_t(IDX_W), tok(KV_W), tok(LANES),
                 pl.BlockSpec((tm // tk, ATT_KV_HEADS * V_AUG, tk), lambda i: (i, 0, 0)), tok_t(GT_ROWS),
                 tok(ML_W), tok(ML_W), tok(ML_W), tok(ML_W)]
    in_specs = [tok(D_MODEL), tok_t(1)] + [full(a) for a in (gmix, wmain, wt, gcq, wqupt, gq, gk, gik, invf)]
    return pl.pallas_call(
        _proj_kernel, grid=grid, in_specs=in_specs, out_specs=out_specs, out_shape=out_shape,
        compiler_params=pltpu.CompilerParams(dimension_semantics=("parallel",), vmem_limit_bytes=VMEM_LIMIT),
        name="proj",
    )(h, pos, gmix, wmain, wt, gcq, wqupt, gq, gk, gik, invf)


def _dsa_kernel(aqp_ref, iqt_ref, gt_ref, gtk_ref, ak_ref, ik_ref, avt_ref, o_ref,
                sc_ref, plane_ref, s_ref, cut_ref, m_ref, acc_ref, kept_ref, tu_ref, above_ref, eq_ref,
                *, n_sel, t):
    qi = pl.program_id(1)
    nk = qi + 1
    k_loc = lax.broadcasted_iota(i32, (t, t), 0)
    q_pos = lax.broadcasted_iota(i32, (t, t), 1) + qi * t
    w_idx = gt_ref[GT_IW:GT_IW + IDX_HEADS, :] * IDX_SCALE

    def for_each_chunk(chunk_work):
        def group(i, carry):
            for j in range(CHUNK_UNROLL):
                chunk_work(CHUNK_UNROLL * i + j)
            return carry
        lax.fori_loop(0, nk // CHUNK_UNROLL, group, 0)
        rem = nk % CHUNK_UNROLL

        @pl.when(rem >= 2)
        def _():
            chunk_work(nk - rem)
            chunk_work(nk - rem + 1)

        @pl.when(rem % 2 == 1)
        def _():
            chunk_work(nk - 1)

    k_max2 = jnp.max(gtk_ref[GT_KN2:GT_KN2 + ATT_KV_HEADS, :], axis=1, keepdims=True)
    bound = jnp.concatenate(
        [jnp.sqrt(gt_ref[GT_QN2 + hh:GT_QN2 + hh + 1, :] * k_max2[hh // GQA:hh // GQA + 1, :])
         for hh in range(ATT_HEADS)], axis=0) * BOUND_SLACK + (BOUND_SLACK - 1.0)

    @pl.when(qi == 0)
    def _():
        plane_ref[...] = jnp.zeros(plane_ref.shape, i32)

    def score_chunk(kc):
        off = pl.multiple_of(kc * t, t)
        ikc = ik_ref[pl.ds(off, t), 0:IDX_DIM]
        s = jnp.zeros((t, t), f32)
        for hh in range(IDX_HEADS):
            lg = _dot(ikc, iqt_ref[hh * IDX_DIM:(hh + 1) * IDX_DIM, :])
            s = s + w_idx[hh:hh + 1, :] * jnp.maximum(lg, 0.0)
        causal = k_loc + kc * t <= q_pos
        sc_ref[kc] = jnp.where(causal, s, -jnp.inf)
        bits = pltpu.bitcast(s, i32)
        key = bits ^ ((bits >> 31) & 0x7FFFFFFF)
        key = jnp.where(bits == INT_MIN, 0, key)
        key = jnp.where(causal, key, INT_MIN)
        u = key ^ INT_MIN
        planes = _bit_transpose32([u[SUBLANES * i:SUBLANES * (i + 1), :] for i in range(KEY_BITS)])
        for b in range(KEY_BITS):
            plane_ref[b, kc] = planes[b]
        rows = ak_ref[pl.ds(off, t), :]
        for hh in range(ATT_HEADS):
            s_ref[hh, kc] = _dot(rows, aqp_ref[hh * Q_PAD:(hh + 1) * Q_PAD, :])

    for_each_chunk(score_chunk)

    def popcount_rows(words):
        pc = lax.population_count(words)
        return jnp.sum(jnp.sum(pc, axis=0).astype(f32), axis=0, keepdims=True)

    n_chunks = plane_ref.shape[1]
    word_chunk = lax.broadcasted_iota(i32, (n_chunks, SUBLANES, t), 0)

    def radix_select(n_used):
        def bit_body(it, carry):
            t_u, above, eq = carry
            ones = eq & plane_ref[it, 0:n_used]
            c1 = popcount_rows(ones)
            ok = above + c1 >= n_sel
            t_u = jnp.where(ok, t_u | lax.shift_left(jnp.int32(1), KEY_BITS - 1 - it), t_u)
            return t_u, jnp.where(ok, above, above + c1), jnp.where(ok, ones, eq ^ ones)

        eq0 = jnp.where(lax.broadcasted_iota(i32, (n_used, SUBLANES, t), 0) < nk, -1, 0)
        t_u, above, eq = lax.fori_loop(0, KEY_BITS, bit_body,
                                       (jnp.zeros((1, t), i32), jnp.zeros((1, t), f32), eq0))
        if n_used < n_chunks:
            eq = jnp.concatenate([eq, jnp.zeros((n_chunks - n_used, SUBLANES, t), i32)], axis=0)
        tu_ref[...] = t_u
        above_ref[...] = above
        eq_ref[...] = eq

    half = max(1, n_chunks // 2)

    @pl.when(nk <= half)
    def _():
        radix_select(half)

    @pl.when(nk > half)
    def _():
        radix_select(n_chunks)

    t_u, above, eq = tu_ref[...], above_ref[...], eq_ref[...]
    cnt_t = above + popcount_rows(eq)
    t_s = t_u ^ INT_MIN

    cut_ref[...] = jnp.full((1, t), 2 ** 30, i32)

    @pl.when(jnp.max(cnt_t) > n_sel)
    def _():
        need = n_sel - above
        first_pos = word_chunk * t + lax.broadcasted_iota(i32, (n_chunks, SUBLANES, t), 1)

        def ties_below(bound_pos):
            n_top = jnp.clip((bound_pos - first_pos + (SUBLANES - 1)) >> 3, 0, KEY_BITS)
            top = lax.shift_right_arithmetic(jnp.full(n_top.shape, INT_MIN, i32), jnp.maximum(n_top - 1, 0))
            return popcount_rows(eq & jnp.where(n_top > 0, top, 0))

        c = jnp.zeros((1, t), i32)
        n_bits = max(1, int(ak_ref.shape[0] - 1).bit_length())
        for b in range(n_bits, -1, -1):
            cand = c | (1 << b)
            c = jnp.where(ties_below(cand) <= need, cand, c)
        cut_ref[...] = c

    def as_f32(m):
        return jnp.where(m, 1.0, 0.0)

    def selected(kc, thr, cut):
        sc = sc_ref[kc]
        idx = k_loc + kc * t
        return ((sc > thr) | ((sc == thr) & (idx < cut))) & (idx <= q_pos)

    def add_value_products(kc, hh, p_bf16):
        g = hh // GQA
        acc_ref[hh] += _dot(avt_ref[kc, g * V_AUG:(g + 1) * V_AUG, :], p_bf16)

    shift_is_safe = jnp.max(bound) <= BOUND_LIMIT
    ones_rows = jnp.ones((kept_ref.shape[0], t), bf16)

    def attention(thr, cut, count_kept):
        acc_ref[...] = jnp.zeros(acc_ref.shape, f32)
        if count_kept:
            kept_ref[...] = jnp.zeros(kept_ref.shape, f32)

        @pl.when(shift_is_safe)
        def _():
            def att_chunk(kc):
                keep = as_f32(selected(kc, thr, cut)).astype(bf16)
                if count_kept:
                    kept_ref[...] += _dot(ones_rows, keep)
                for hh in range(ATT_HEADS):
                    add_value_products(kc, hh, jnp.exp2(s_ref[hh, kc]).astype(bf16) * keep)
            for_each_chunk(att_chunk)

        @pl.when(jnp.logical_not(shift_is_safe))
        def _():
            def max_body(kc, mx):
                bias = jnp.where(selected(kc, thr, cut), 0.0, NEG_BIG)
                rows = [jnp.max(_fold_rows(s_ref[hh, kc] + bias, jnp.maximum), axis=0, keepdims=True)
                        for hh in range(ATT_HEADS)]
                return jnp.maximum(mx, jnp.concatenate(rows, axis=0))
            m_ref[...] = lax.fori_loop(0, nk, max_body, jnp.full((ATT_HEADS, t), NEG_BIG, f32))

            def att_body(kc, carry):
                sel = selected(kc, thr, cut)
                if count_kept:
                    kept_ref[...] += _dot(ones_rows, as_f32(sel).astype(bf16))
                bias = jnp.where(sel, 0.0, NEG_BIG)
                for hh in range(ATT_HEADS):
                    p = jnp.exp2(s_ref[hh, kc] + bias - m_ref[hh:hh + 1, :])
                    add_value_products(kc, hh, p.astype(bf16))
                return carry
            lax.fori_loop(0, nk, att_body, 0)

    tb = jnp.where(t_s < 0, t_s ^ 0x7FFFFFFF, t_s)
    thr0 = jnp.where(t_s == INT_MIN, -jnp.inf, pltpu.bitcast(tb, f32))
    cut0 = cut_ref[...]
    attention(thr0, cut0, True)
    n_causal = (lax.broadcasted_iota(i32, (1, t), 1) + (qi * t + 1)).astype(f32)
    kept_is_off = kept_ref[0:1, :] != jnp.minimum(n_causal, float(n_sel))

    @pl.when(jnp.max(as_f32(kept_is_off)) > 0.0)
    def _():
        def over_chunks(stat_fn, inits, ops):
            def body(kc, parts):
                vals = stat_fn(sc_ref[kc], k_loc + kc * t)
                return tuple(op(p, _fold_rows(v, op)) for p, v, op in zip(parts, vals, ops))
            parts = lax.fori_loop(0, nk, body, tuple(jnp.full((SUBLANES, t), v, f32) for v in inits))
            reducers = {jnp.add: jnp.sum, jnp.minimum: jnp.min, jnp.maximum: jnp.max}
            return tuple(reducers[op](p, axis=0, keepdims=True) for p, op in zip(parts, ops))

        def counts(thr):
            return over_chunks(
                lambda sc, idx: (as_f32(sc > thr), as_f32(sc == thr), as_f32((sc == thr) & (idx < cut0))),
                (0.0, 0.0, 0.0), (jnp.add, jnp.add, jnp.add))

        def polish_cond(state):
            _, n_above, n_equal, _ = state
            return jnp.max(as_f32((n_above >= n_sel) | (n_above + n_equal < n_sel))) > 0.0

        def polish_body(state):
            thr, n_above, n_equal, _ = state
            up, down = over_chunks(
                lambda sc, idx: (jnp.where(sc > thr, sc, jnp.inf), jnp.where(sc < thr, sc, -jnp.inf)),
                (jnp.inf, -jnp.inf), (jnp.minimum, jnp.maximum))
            thr = jnp.where(n_above >= n_sel, up, jnp.where(n_above + n_equal < n_sel, down, thr))
            return (thr,) + counts(thr)

        thr, n_above, n_equal, n_kept = lax.while_loop(polish_cond, polish_body, (thr0,) + counts(thr0))
        need = n_sel - n_above
        cut_is_off = n_kept != jnp.minimum(n_equal, need)

        @pl.when(jnp.max(as_f32(cut_is_off)) > 0.0)
        def _():
            c = jnp.zeros((1, t), i32)
            n_bits = max(1, int(ak_ref.shape[0] - 1).bit_length())
            for b in range(n_bits, -1, -1):
                cand = c | (1 << b)
                (below,) = over_chunks(lambda sc, idx: (as_f32((sc == thr) & (idx < cand)),), (0.0,), (jnp.add,))
                c = jnp.where(below <= need, cand, c)
            cut_ref[...] = c

        attention(thr, cut_ref[...], False)

    att_t = jnp.concatenate(
        [acc_ref[hh, 0:HEAD_DIM, :] / acc_ref[hh, HEAD_DIM:HEAD_DIM + 1, :] for hh in range(ATT_HEADS)], axis=0)
    o_ref[...] = att_t.T.astype(o_ref.dtype)


def _dsa_call(aqp, iqt, gt, ak, ik, avt, n_sel, b, s, t):
    nq = s // t

    def qblk_t(r):
        return pl.BlockSpec((r, t), lambda bi, qi: (0, bi * nq + qi))

    def kblk(w):
        return pl.BlockSpec((None, s, w), lambda bi, qi: (bi, 0, 0))

    return pl.pallas_call(
        functools.partial(_dsa_kernel, n_sel=n_sel, t=t),
        grid=(b, nq),
        in_specs=[qblk_t(ATT_HEADS * Q_PAD), qblk_t(IDX_W), qblk_t(GT_ROWS),
                  pl.BlockSpec((GT_ROWS, s), lambda bi, qi: (0, bi)), kblk(KV_W), kblk(LANES),
                  pl.BlockSpec((nq, ATT_KV_HEADS * V_AUG, t), lambda bi, qi: (bi, 0, 0))],
        out_specs=pl.BlockSpec((None, t, ATT_W), lambda bi, qi: (bi, qi, 0)),
        out_shape=jax.ShapeDtypeStruct((b, s, ATT_W), bf16),
        scratch_shapes=[
            pltpu.VMEM((nq, t, t), f32),
            pltpu.VMEM((KEY_BITS, nq, SUBLANES, t), i32),
            pltpu.VMEM((ATT_HEADS, nq, t, t), f32),
            pltpu.VMEM((1, t), i32),
            pltpu.VMEM((ATT_HEADS, t), f32),
            pltpu.VMEM((ATT_HEADS, V_AUG, t), f32),
            pltpu.VMEM((V_AUG - HEAD_DIM, t), f32),
            pltpu.VMEM((1, t), i32),
            pltpu.VMEM((1, t), f32),
            pltpu.VMEM((nq, SUBLANES, t), i32),
        ],
        compiler_params=pltpu.CompilerParams(dimension_semantics=("parallel", "arbitrary"),
                                             vmem_limit_bytes=VMEM_LIMIT),
        name="dsa",
    )(aqp, iqt, gt, gt, ak, ik, avt)


def _mlstm_kernel(mq_ref, mk_ref, mv_ref, mo_ref, gt_ref, gb_ref, cwq_ref, cwk_ref, cbq_ref, cbk_ref, gmh_ref,
                  o_ref, *, chunk):
    hd = pl.program_id(1)
    seq = mq_ref.shape[0]
    n_chunks = seq // chunk
    row = lax.broadcasted_iota(i32, (chunk, chunk), 0)
    col = lax.broadcasted_iota(i32, (chunk, chunk), 1)
    tril = col <= row
    eye = col == row
    row_t = lax.broadcasted_iota(i32, (SUBLANES, ML_DIM), 0)

    def conv_silu(x_ref, w_ref, b_ref, c):
        t0 = c * chunk
        cur = x_ref[t0:t0 + chunk, :]
        acc = cur * w_ref[CONV_W - 1:CONV_W, :] + b_ref[...]
        for j in range(1, CONV_W):
            if c > 0:
                shifted = x_ref[t0 - j:t0 - j + chunk, :]
            else:
                rolled = pltpu.roll(cur, j, 0)
                first = jnp.where(row_t >= j, rolled[0:SUBLANES], 0.0)
                shifted = jnp.concatenate([first, rolled[SUBLANES:]], axis=0)
            acc = acc + shifted * w_ref[CONV_W - 1 - j:CONV_W - j, :]
        return acc * _sigmoid(acc)

    c_state = jnp.zeros((ML_DIM, ML_DIM), f32)
    n_state = jnp.zeros((1, ML_DIM), f32)
    m_state = jnp.zeros((1, 1), f32)
    li_all = gt_ref[pl.ds(GT_MI + hd, 1), :] + gb_ref[pl.ds(hd, 1), :]
    f_all = gt_ref[pl.ds(GT_MF + hd, 1), :] + gb_ref[pl.ds(ML_HEADS + hd, 1), :]
    lf_all = -(jnp.maximum(-f_all, 0.0) + jnp.log1p(jnp.exp(-jnp.abs(f_all))))
    for c in range(n_chunks):
        t0 = c * chunk
        q = conv_silu(mq_ref, cwq_ref, cbq_ref, c) * (ML_DIM ** -0.5)
        k = conv_silu(mk_ref, cwk_ref, cbk_ref, c)
        qb = q.astype(bf16)
        kb = k.astype(bf16)
        vb = mv_ref[t0:t0 + chunk, :]
        li_row = li_all[:, t0:t0 + chunk]
        lf_row = lf_all[:, t0:t0 + chunk]
        b_col = jnp.sum(jnp.where(tril, lf_row, 0.0), axis=1, keepdims=True)
        b_row = jnp.sum(jnp.where(eye, b_col, 0.0), axis=0, keepdims=True)
        li_col = jnp.sum(jnp.where(eye, li_row, 0.0), axis=1, keepdims=True)
        b_last = jnp.sum(lf_row, axis=1, keepdims=True)

        log_d = jnp.where(tril, b_col - b_row + li_row, -jnp.inf)
        inter = b_col + m_state
        m_t = jnp.maximum(inter, jnp.max(log_d, axis=1, keepdims=True))
        dmat = jnp.exp(log_d - m_t)
        inter_w = jnp.exp(inter - m_t)
        qk = _dot_nt(qb, kb) * dmat
        num = inter_w * _dot(qb, c_state.astype(bf16)) + _dot(qk.astype(bf16), vb)
        den = inter_w * jnp.sum(q * n_state, axis=1, keepdims=True) + jnp.sum(qk, axis=1, keepdims=True)
        h_t = num / jnp.maximum(jnp.abs(den), jnp.exp(-m_t))

        log_g = b_last - b_col + li_col
        m_new = jnp.maximum(b_last + m_state, jnp.max(log_g, axis=0, keepdims=True))
        g = jnp.exp(log_g - m_new)
        decay = jnp.exp(b_last + m_state - m_new)
        gk = g * k
        c_state = decay * c_state + _dot(gk.T.astype(bf16), vb)
        n_state = decay * n_state + jnp.sum(gk, axis=0, keepdims=True)
        m_state = m_new

        hn = h_t * lax.rsqrt(jnp.mean(h_t * h_t, axis=1, keepdims=True) + EPS) * gmh_ref[...]
        o_ref[t0:t0 + chunk, :] = (_sigmoid(mo_ref[t0:t0 + chunk, :]) * hn).astype(o_ref.dtype)


def _mlstm_call(mq, mk, mv, mo, gt, gb, conv_w, conv_b, gmh):
    b, s, _ = mq.shape
    chunk = min(ML_CHUNK, s)

    def head_blk():
        return pl.BlockSpec((None, s, ML_DIM), lambda bi, hi: (bi, 0, hi))

    return pl.pallas_call(
        functools.partial(_mlstm_kernel, chunk=chunk),
        grid=(b, ML_HEADS),
        in_specs=[head_blk(), head_blk(), head_blk(), head_blk(),
                  pl.BlockSpec((GT_ROWS, s), lambda bi, hi: (0, bi)),
                  pl.BlockSpec((2 * ML_HEADS, 1), lambda bi, hi: (0, 0)),
                  pl.BlockSpec((CONV_W, ML_DIM), lambda bi, hi: (0, hi)),
                  pl.BlockSpec((CONV_W, ML_DIM), lambda bi, hi: (0, ML_HEADS + hi)),
                  pl.BlockSpec((1, ML_DIM), lambda bi, hi: (0, hi)),
                  pl.BlockSpec((1, ML_DIM), lambda bi, hi: (0, ML_HEADS + hi)),
                  pl.BlockSpec((None, 1, ML_DIM), lambda bi, hi: (hi, 0, 0))],
        out_specs=head_blk(),
        out_shape=jax.ShapeDtypeStruct((b, s, ML_W), bf16),
        compiler_params=pltpu.CompilerParams(dimension_semantics=("parallel", "parallel"),
                                             vmem_limit_bytes=VMEM_LIMIT),
        name="mlstm",
    )(mq, mk, mv, mo, gt, gb, conv_w, conv_w, conv_b, conv_b, gmh)


def _mix_kernel(h_ref, att_ref, hm_ref, p_ref, wo_ref, gmlp_ref, w1_ref, w2_ref, gple_ref, wg_ref, bg_ref, wp_ref,
                o_ref):
    def rms(v, g_ref):
        return (v * lax.rsqrt(jnp.mean(v * v, axis=-1, keepdims=True) + EPS) * g_ref[...]).astype(bf16)

    mixed = _dot(att_ref[...], wo_ref[0:ATT_W, :]) + _dot(hm_ref[...], wo_ref[ATT_W:ATT_W + ML_W, :])
    h1 = h_ref[...] + mixed
    xn = rms(h1, gmlp_ref)
    mlp = None
    for f in range(D_FF // FF_CHUNK):
        u = jnp.maximum(_dot(xn, w1_ref[:, f * FF_CHUNK:(f + 1) * FF_CHUNK]), 0.0)
        part = _dot((u * u).astype(bf16), w2_ref[f * FF_CHUNK:(f + 1) * FF_CHUNK, :])
        mlp = part if mlp is None else mlp + part
    h2 = h1 + mlp
    gate = _sigmoid(_dot(rms(h2, gple_ref), wg_ref[...]) + bg_ref[...])
    o_ref[...] = h2 + gate * _dot(p_ref[...].astype(bf16), wp_ref[...])


def _mix_call(h, att, hm, p, wo, gmlp, w1, w2, gple, wg, bg, wp):
    n = h.shape[0]
    tm = min(MIX_TM, n)

    def tok(w):
        return pl.BlockSpec((tm, w), lambda i: (i, 0))

    def full(a):
        return pl.BlockSpec(a.shape, lambda i: (0,) * a.ndim, pipeline_mode=pl.Buffered(1))

    return pl.pallas_call(
        _mix_kernel, grid=(n // tm,),
        in_specs=[tok(D_MODEL), tok(ATT_W), tok(ML_W), tok(PLE_DIM)]
                 + [full(a) for a in (wo, gmlp, w1, w2, gple, wg, bg, wp)],
        out_specs=tok(D_MODEL),
        out_shape=jax.ShapeDtypeStruct((n, D_MODEL), f32),
        compiler_params=pltpu.CompilerParams(dimension_semantics=("parallel",), vmem_limit_bytes=VMEM_LIMIT),
        name="mix",
    )(h, att, hm, p, wo, gmlp, w1, w2, gple, wg, bg, wp)


def kernel(x, p, positions, g_mix, w_in, g_cq, w_q_up, w_iq_up, g_qn, g_kn, g_ik, conv_w, conv_b, i_bias, f_bias,
           g_mh, w_out, g_mlp, w_ff1, w_ff2, g_ple, w_ple_gate, b_ple_gate, w_ple):
    b, s, d = x.shape
    n = b * s
    depth = p.shape[0]
    n_sel = min(TOPK_MAX, s // 4)
    t = min(DSA_T, s)
    h = x.reshape(n, d)
    pos = positions.reshape(1, n).astype(i32)
    invf = (ROPE_THETA ** (-(jnp.arange(ROT_HALF, dtype=f32) * 2.0) / ROT_DIM))[:, None]

    split = np.cumsum(IN_SIZES)[:-1].tolist()
    for i in range(depth):
        c_q, a_k, a_v, i_k, i_w, m_q, m_k, m_v, m_o, m_i, m_f = jnp.split(w_in[i], split, axis=1)
        wmain = jnp.concatenate([c_q, m_q, m_k, m_v, m_o], axis=1).astype(bf16)
        wt = jnp.concatenate([a_k, i_k, a_v, i_w, m_i, m_f], axis=1).T.astype(bf16)
        wqupt = jnp.concatenate([w_q_up[i], w_iq_up[i]], axis=1).T.astype(bf16)
        aqt, iqt, ak, ik, avt, gt, mq, mk, mv, mo = _proj_call(
            h, pos, g_mix[i][None, :], wmain, wt, g_cq[i][None, :], wqupt,
            g_qn[i][:, None], g_kn[i][:, None], g_ik[i][:, None], invf, t)

        def b3(a):
            return a.reshape(b, s, a.shape[-1])

        att = _dsa_call(aqt, iqt, gt, b3(ak), b3(ik), avt, n_sel, b, s, t)
        gb = jnp.concatenate([i_bias[i], f_bias[i]])[:, None]
        hm = _mlstm_call(b3(mq), b3(mk), b3(mv), b3(mo), gt, gb, conv_w[i], conv_b[i][None, :],
                         g_mh[i][:, None, :])
        h = _mix_call(h, att.reshape(n, ATT_W), hm.reshape(n, ML_W), p[i].reshape(n, PLE_DIM),
                      w_out[i].astype(bf16), g_mlp[i][None, :], w_ff1[i].astype(bf16), w_ff2[i].astype(bf16),
                      g_ple[i][None, :], w_ple_gate[i].astype(bf16), b_ple_gate[i][None, :], w_ple[i].astype(bf16))
    return h.reshape(b, s, d)
```

```python
import functools

import numpy as np
import jax
import jax.numpy as jnp
from jax import lax
from jax.experimental import pallas as pl
from jax.experimental.pallas import tpu as pltpu

D_MODEL = 1024
PLE_DIM = 256
ATT_HEADS = 8
ATT_KV_HEADS = 2
HEAD_DIM = 64
Q_RANK = 256
IDX_HEADS = 8
IDX_DIM = 64
TOPK_MAX = 256
ML_HEADS = 4
ML_DIM = 128
CONV_W = 4
D_FF = 4 * D_MODEL
ROPE_THETA = 500000.0
ROT_DIM = HEAD_DIM // 4
ROT_HALF = ROT_DIM // 2
EPS = 1e-6

ATT_W = ATT_HEADS * HEAD_DIM
KV_W = ATT_KV_HEADS * HEAD_DIM
ML_W = ML_HEADS * ML_DIM
IDX_W = IDX_HEADS * IDX_DIM
IN_SIZES = (Q_RANK, KV_W, KV_W, IDX_DIM, IDX_HEADS, ML_W, ML_W, ML_W, ML_W, ML_HEADS, ML_HEADS)
IDX_SCALE = (IDX_HEADS ** -0.5) * (IDX_DIM ** -0.5)
ATT_SCALE = HEAD_DIM ** -0.5
GQA = ATT_HEADS // ATT_KV_HEADS

LANES = 128
SUBLANES = 8
OFF_CQ = 0
OFF_MQ = Q_RANK
OFF_MK, OFF_MV, OFF_MO = OFF_MQ + ML_W, OFF_MQ + 2 * ML_W, OFF_MQ + 3 * ML_W
MAIN_W = OFF_MQ + 4 * ML_W
ROW_AK, ROW_IK, ROW_AV = 0, KV_W, KV_W + IDX_DIM
ROW_GT = ROW_AV + KV_W
GT_IW, GT_MI, GT_MF = 0, IDX_HEADS, IDX_HEADS + ML_HEADS
GT_PROJ = IDX_HEADS + 2 * ML_HEADS
GT_QN2 = GT_PROJ
GT_KN2 = GT_QN2 + ATT_HEADS
GT_ROWS = 32
T_ROWS = ROW_GT + GT_PROJ
Q_PAD = 2 * HEAD_DIM
V_AUG = HEAD_DIM + 16
KEY_BITS = 32
CHUNK_UNROLL = 4
IDX_SPLIT = 4
LOG2E = 1.4426950408889634
BOUND_SLACK = 1.001
BOUND_LIMIT = 40.0

PROJ_TM = 512
MIX_TM = 512
FF_CHUNK = 1024
DSA_T = 256
ML_CHUNK = 256
NEG_BIG = -1e30
VMEM_LIMIT = 56 * 1024 * 1024

_NT = (((1,), (1,)), ((), ()))

f32 = jnp.float32
bf16 = jnp.bfloat16
i32 = jnp.int32
INT_MIN = -2 ** 31


def _dot(a, b):
    return jnp.dot(a, b, preferred_element_type=f32)


def _dot_nt(a, b):
    return lax.dot_general(a, b, _NT, preferred_element_type=f32)


def _sigmoid(x):
    return 0.5 * jnp.tanh(0.5 * x) + 0.5


def _bit_transpose32(words):
    a = list(words)
    j, m = 16, 0x0000FFFF
    while j:
        mask = int(np.array(m, np.uint32).view(np.int32))
        k = 0
        while k < 32:
            tmp = (a[k] ^ lax.shift_right_logical(a[k + j], jnp.int32(j))) & mask
            a[k] = a[k] ^ tmp
            a[k + j] = a[k + j] ^ lax.shift_left(tmp, jnp.int32(j))
            k = (k + j + 1) & ~j
        j >>= 1
        m = (m ^ (m << j)) & 0xFFFFFFFF
    return a


def _fold_rows(x, op):
    x = x.reshape(x.shape[0] // SUBLANES, SUBLANES, x.shape[1])
    while x.shape[0] > 1:
        half = x.shape[0] // 2
        x = op(x[:half], x[half:])
    return x[0]


def _proj_kernel(h_ref, pos_ref, gmix_ref, wmain_ref, wt_ref, gcq_ref, wqupt_ref, gq_ref, gk_ref, gik_ref, invf_ref,
                 aqp_ref, iqt_ref, ak_ref, ik_ref, avt_ref, gt_ref, mq_ref, mk_ref, mv_ref, mo_ref):
    x = h_ref[...]
    ms = jnp.mean(x * x, axis=-1, keepdims=True)
    xn = (x * lax.rsqrt(ms + EPS) * gmix_ref[...]).astype(bf16)
    proj = _dot(xn, wmain_ref[...])
    pt = _dot_nt(wt_ref[...], xn)

    ang = invf_ref[...] * pos_ref[...].astype(f32)
    cos_t = jnp.cos(ang)
    sin_t = jnp.sin(ang)

    def rope_t(blk):
        x1 = blk[0:ROT_HALF]
        x2 = blk[ROT_HALF:ROT_DIM]
        return jnp.concatenate([x1 * cos_t - x2 * sin_t, x2 * cos_t + x1 * sin_t, blk[ROT_DIM:]], axis=0)

    def norm_t(blk, g_ref):
        return blk * lax.rsqrt(jnp.mean(blk * blk, axis=0, keepdims=True) + EPS) * g_ref[...]

    def head(a, j):
        return a[j * HEAD_DIM:(j + 1) * HEAD_DIM]

    cq = proj[:, OFF_CQ:OFF_CQ + Q_RANK]
    cqn = (cq * lax.rsqrt(jnp.mean(cq * cq, axis=-1, keepdims=True) + EPS) * gcq_ref[...]).astype(bf16)
    qqt = _dot_nt(wqupt_ref[...], cqn)
    tm = x.shape[0]

    def sq_norm_rows(blk_bf16):
        v = blk_bf16.astype(f32)
        return jnp.sum(v * v, axis=0, keepdims=True)

    zero_h = jnp.zeros((HEAD_DIM, tm), bf16)
    q_blocks, qn2 = [], []
    for j in range(ATT_HEADS):
        qb = (rope_t(norm_t(head(qqt, j), gq_ref)) * (ATT_SCALE * LOG2E)).astype(bf16)
        qn2.append(sq_norm_rows(qb))
        q_blocks += [qb if g == j // GQA else zero_h for g in range(ATT_KV_HEADS)]
    aqp_ref[...] = jnp.concatenate(q_blocks, axis=0)
    iqt = jnp.concatenate([rope_t(head(qqt, ATT_HEADS + j)) for j in range(IDX_HEADS)], axis=0)
    iqt_ref[...] = iqt.astype(bf16)

    akt = jnp.concatenate([rope_t(norm_t(head(pt, j), gk_ref)) for j in range(ATT_KV_HEADS)], axis=0)
    akb = akt.astype(bf16)
    kn2 = [sq_norm_rows(head(akb, g)) for g in range(ATT_KV_HEADS)]
    ak_ref[...] = akt.T.astype(bf16)
    ikt = rope_t(norm_t(pt[ROW_IK:ROW_IK + IDX_DIM], gik_ref))
    ikt = jnp.concatenate([ikt, jnp.zeros((LANES - IDX_DIM, tm), f32)], axis=0)
    ik_ref[...] = ikt.T.astype(bf16)
    avt = pt[ROW_AV:ROW_AV + KV_W].astype(bf16)
    ones = jnp.ones((V_AUG - HEAD_DIM, tm), bf16)
    avaug = jnp.concatenate([blk for g in range(ATT_KV_HEADS) for blk in (head(avt, g), ones)], axis=0)
    tk = avt_ref.shape[2]
    for j in range(avt_ref.shape[0]):
        avt_ref[j] = avaug[:, j * tk:(j + 1) * tk]
    gt_ref[...] = jnp.concatenate(
        [pt[ROW_GT:ROW_GT + GT_PROJ]] + qn2 + kn2
        + [jnp.zeros((GT_ROWS - GT_KN2 - ATT_KV_HEADS, tm), f32)], axis=0)

    mq_ref[...] = proj[:, OFF_MQ:OFF_MQ + ML_W]
    mk_ref[...] = proj[:, OFF_MK:OFF_MK + ML_W]
    mv_ref[...] = proj[:, OFF_MV:OFF_MV + ML_W].astype(bf16)
    mo_ref[...] = proj[:, OFF_MO:OFF_MO + ML_W]


def _proj_call(h, pos, gmix, wmain, wt, gcq, wqupt, gq, gk, gik, invf, tk):
    n = h.shape[0]
    tm = min(PROJ_TM, n)
    grid = (n // tm,)

    def tok(w):
        return pl.BlockSpec((tm, w), lambda i: (i, 0))

    def tok_t(r):
        return pl.BlockSpec((r, tm), lambda i: (0, i))

    def full(a):
        return pl.BlockSpec(a.shape, lambda i: (0,) * a.ndim)

    out_shape = [
        jax.ShapeDtypeStruct((ATT_HEADS * Q_PAD, n), bf16),
        jax.ShapeDtypeStruct((IDX_W, n), bf16),
        jax.ShapeDtypeStruct((n, KV_W), bf16),
        jax.ShapeDtypeStruct((n, LANES), bf16),
        jax.ShapeDtypeStruct((n // tk, ATT_KV_HEADS * V_AUG, tk), bf16),
        jax.ShapeDtypeStruct((GT_ROWS, n), f32),
        jax.ShapeDtypeStruct((n, ML_W), f32),
        jax.ShapeDtypeStruct((n, ML_W), f32),
        jax.ShapeDtypeStruct((n, ML_W), bf16),
        jax.ShapeDtypeStruct((n, ML_W), f32),
    ]
    out_specs = [tok_t(ATT_HEADS * Q_PAD), tok_t(IDX_W), tok(KV_W), tok(LANES),
                 pl.BlockSpec((tm // tk, ATT_KV_HEADS * V_AUG, tk), lambda i: (i, 0, 0)), tok_t(GT_ROWS),
                 tok(ML_W), tok(ML_W), tok(ML_W), tok(ML_W)]
    in_specs = [tok(D_MODEL), tok_t(1)] + [full(a) for a in (gmix, wmain, wt, gcq, wqupt, gq, gk, gik, invf)]
    return pl.pallas_call(
        _proj_kernel, grid=grid, in_specs=in_specs, out_specs=out_specs, out_shape=out_shape,
        compiler_params=pltpu.CompilerParams(dimension_semantics=("parallel",), vmem_limit_bytes=VMEM_LIMIT),
        name="proj",
    )(h, pos, gmix, wmain, wt, gcq, wqupt, gq, gk, gik, invf)


def _dsa_kernel(aqp_ref, iqt_ref, gt_ref, gtk_ref, ak_ref, ik_ref, avt_ref, o_ref,
                sc_ref, plane_ref, s_ref, cut_ref, m_ref, acc_ref, kept_ref, tu_ref, above_ref, eq_ref,
                *, n_sel, t):
    qi = pl.program_id(1)
    nk = qi + 1
    k_loc = lax.broadcasted_iota(i32, (t, t), 0)
    q_pos = lax.broadcasted_iota(i32, (t, t), 1) + qi * t
    w_idx = gt_ref[GT_IW:GT_IW + IDX_HEADS, :] * IDX_SCALE

    def for_each_chunk(chunk_work):
        def group(i, carry):
            for j in range(CHUNK_UNROLL):
                chunk_work(CHUNK_UNROLL * i + j)
            return carry
        lax.fori_loop(0, nk // CHUNK_UNROLL, group, 0)
        rem = nk % CHUNK_UNROLL

        @pl.when(rem >= 2)
        def _():
            chunk_work(nk - rem)
            chunk_work(nk - rem + 1)

        @pl.when(rem % 2 == 1)
        def _():
            chunk_work(nk - 1)

    k_max2 = jnp.max(gtk_ref[GT_KN2:GT_KN2 + ATT_KV_HEADS, :], axis=1, keepdims=True)
    bound = jnp.concatenate(
        [jnp.sqrt(gt_ref[GT_QN2 + hh:GT_QN2 + hh + 1, :] * k_max2[hh // GQA:hh // GQA + 1, :])
         for hh in range(ATT_HEADS)], axis=0) * BOUND_SLACK + (BOUND_SLACK - 1.0)

    @pl.when(qi == 0)
    def _():
        plane_ref[...] = jnp.zeros(plane_ref.shape, i32)

    def score_chunk(kc):
        off = pl.multiple_of(kc * t, t)
        slices = []
        for part in range(IDX_SPLIT):
            rows = t // IDX_SPLIT
            ikc = ik_ref[pl.ds(pl.multiple_of(off + part * rows, rows), rows), 0:IDX_DIM]
            acc = jnp.zeros((rows, t), f32)
            for hh in range(IDX_HEADS):
                lg = _dot(ikc, iqt_ref[hh * IDX_DIM:(hh + 1) * IDX_DIM, :])
                acc = acc + w_idx[hh:hh + 1, :] * jnp.maximum(lg, 0.0)
            slices.append(acc)
        s = jnp.concatenate(slices, axis=0)
        causal = k_loc + kc * t <= q_pos
        sc_ref[kc] = jnp.where(causal, s, -jnp.inf)
        bits = pltpu.bitcast(s, i32)
        key = bits ^ ((bits >> 31) & 0x7FFFFFFF)
        key = jnp.where(bits == INT_MIN, 0, key)
        key = jnp.where(causal, key, INT_MIN)
        u = key ^ INT_MIN
        planes = _bit_transpose32([u[SUBLANES * i:SUBLANES * (i + 1), :] for i in range(KEY_BITS)])
        for b in range(KEY_BITS):
            plane_ref[b, kc] = planes[b]
        rows = ak_ref[pl.ds(off, t), :]
        for hh in range(ATT_HEADS):
            s_ref[hh, kc] = _dot(rows, aqp_ref[hh * Q_PAD:(hh + 1) * Q_PAD, :])

    for_each_chunk(score_chunk)

    def popcount_rows(words):
        pc = lax.population_count(words)
        return jnp.sum(jnp.sum(pc, axis=0).astype(f32), axis=0, keepdims=True)

    n_chunks = plane_ref.shape[1]
    word_chunk = lax.broadcasted_iota(i32, (n_chunks, SUBLANES, t), 0)

    def radix_select(n_used):
        def bit_body(it, carry):
            t_u, above, eq = carry
            ones = eq & plane_ref[it, 0:n_used]
            c1 = popcount_rows(ones)
            ok = above + c1 >= n_sel
            t_u = jnp.where(ok, t_u | lax.shift_left(jnp.int32(1), KEY_BITS - 1 - it), t_u)
            return t_u, jnp.where(ok, above, above + c1), jnp.where(ok, ones, eq ^ ones)

        eq0 = jnp.where(lax.broadcasted_iota(i32, (n_used, SUBLANES, t), 0) < nk, -1, 0)
        t_u, above, eq = lax.fori_loop(0, KEY_BITS, bit_body,
                                       (jnp.zeros((1, t), i32), jnp.zeros((1, t), f32), eq0))
        if n_used < n_chunks:
            eq = jnp.concatenate([eq, jnp.zeros((n_chunks - n_used, SUBLANES, t), i32)], axis=0)
        tu_ref[...] = t_u
        above_ref[...] = above
        eq_ref[...] = eq

    half = max(1, n_chunks // 2)

    @pl.when(nk <= half)
    def _():
        radix_select(half)

    @pl.when(nk > half)
    def _():
        radix_select(n_chunks)

    t_u, above, eq = tu_ref[...], above_ref[...], eq_ref[...]
    cnt_t = above + popcount_rows(eq)
    t_s = t_u ^ INT_MIN

    cut_ref[...] = jnp.full((1, t), 2 ** 30, i32)

    @pl.when(jnp.max(cnt_t) > n_sel)
    def _():
        need = n_sel - above
        first_pos = word_chunk * t + lax.broadcasted_iota(i32, (n_chunks, SUBLANES, t), 1)

        def ties_below(bound_pos):
            n_top = jnp.clip((bound_pos - first_pos + (SUBLANES - 1)) >> 3, 0, KEY_BITS)
            top = lax.shift_right_arithmetic(jnp.full(n_top.shape, INT_MIN, i32), jnp.maximum(n_top - 1, 0))
            return popcount_rows(eq & jnp.where(n_top > 0, top, 0))

        c = jnp.zeros((1, t), i32)
        n_bits = max(1, int(ak_ref.shape[0] - 1).bit_length())
        for b in range(n_bits, -1, -1):
            cand = c | (1 << b)
            c = jnp.where(ties_below(cand) <= need, cand, c)
        cut_ref[...] = c

    def as_f32(m):
        return jnp.where(m, 1.0, 0.0)

    def selected(kc, thr, cut):
        sc = sc_ref[kc]
        idx = k_loc + kc * t
        return ((sc > thr) | ((sc == thr) & (idx < cut))) & (idx <= q_pos)

    def add_value_products(kc, hh, p_bf16):
        g = hh // GQA
        acc_ref[hh] += _dot(avt_ref[kc, g * V_AUG:(g + 1) * V_AUG, :], p_bf16)

    shift_is_safe = jnp.max(bound) <= BOUND_LIMIT
    ones_rows = jnp.ones((kept_ref.shape[0], t), bf16)

    def attention(thr, cut, count_kept):
        acc_ref[...] = jnp.zeros(acc_ref.shape, f32)
        if count_kept:
            kept_ref[...] = jnp.zeros(kept_ref.shape, f32)

        @pl.when(shift_is_safe)
        def _():
            def att_chunk(kc):
                keep = as_f32(selected(kc, thr, cut)).astype(bf16)
                if count_kept:
                    kept_ref[...] += _dot(ones_rows, keep)
                for hh in range(ATT_HEADS):
                    add_value_products(kc, hh, jnp.exp2(s_ref[hh, kc]).astype(bf16) * keep)
            for_each_chunk(att_chunk)

        @pl.when(jnp.logical_not(shift_is_safe))
        def _():
            def max_body(kc, mx):
                bias = jnp.where(selected(kc, thr, cut), 0.0, NEG_BIG)
                rows = [jnp.max(_fold_rows(s_ref[hh, kc] + bias, jnp.maximum), axis=0, keepdims=True)
                        for hh in range(ATT_HEADS)]
                return jnp.maximum(mx, jnp.concatenate(rows, axis=0))
            m_ref[...] = lax.fori_loop(0, nk, max_body, jnp.full((ATT_HEADS, t), NEG_BIG, f32))

            def att_body(kc, carry):
                sel = selected(kc, thr, cut)
                if count_kept:
                    kept_ref[...] += _dot(ones_rows, as_f32(sel).astype(bf16))
                bias = jnp.where(sel, 0.0, NEG_BIG)
                for hh in range(ATT_HEADS):
                    p = jnp.exp2(s_ref[hh, kc] + bias - m_ref[hh:hh + 1, :])
                    add_value_products(kc, hh, p.astype(bf16))
                return carry
            lax.fori_loop(0, nk, att_body, 0)

    tb = jnp.where(t_s < 0, t_s ^ 0x7FFFFFFF, t_s)
    thr0 = jnp.where(t_s == INT_MIN, -jnp.inf, pltpu.bitcast(tb, f32))
    cut0 = cut_ref[...]
    attention(thr0, cut0, True)
    n_causal = (lax.broadcasted_iota(i32, (1, t), 1) + (qi * t + 1)).astype(f32)
    kept_is_off = kept_ref[0:1, :] != jnp.minimum(n_causal, float(n_sel))

    @pl.when(jnp.max(as_f32(kept_is_off)) > 0.0)
    def _():
        def over_chunks(stat_fn, inits, ops):
            def body(kc, parts):
                vals = stat_fn(sc_ref[kc], k_loc + kc * t)
                return tuple(op(p, _fold_rows(v, op)) for p, v, op in zip(parts, vals, ops))
            parts = lax.fori_loop(0, nk, body, tuple(jnp.full((SUBLANES, t), v, f32) for v in inits))
            reducers = {jnp.add: jnp.sum, jnp.minimum: jnp.min, jnp.maximum: jnp.max}
            return tuple(reducers[op](p, axis=0, keepdims=True) for p, op in zip(parts, ops))

        def counts(thr):
            return over_chunks(
                lambda sc, idx: (as_f32(sc > thr), as_f32(sc == thr), as_f32((sc == thr) & (idx < cut0))),
                (0.0, 0.0, 0.0), (jnp.add, jnp.add, jnp.add))

        def polish_cond(state):
            _, n_above, n_equal, _ = state
            return jnp.max(as_f32((n_above >= n_sel) | (n_above + n_equal < n_sel))) > 0.0

        def polish_body(state):
            thr, n_above, n_equal, _ = state
            up, down = over_chunks(
                lambda sc, idx: (jnp.where(sc > thr, sc, jnp.inf), jnp.where(sc < thr, sc, -jnp.inf)),
                (jnp.inf, -jnp.inf), (jnp.minimum, jnp.maximum))
            thr = jnp.where(n_above >= n_sel, up, jnp.where(n_above + n_equal < n_sel, down, thr))
            return (thr,) + counts(thr)

        thr, n_above, n_equal, n_kept = lax.while_loop(polish_cond, polish_body, (thr0,) + counts(thr0))
        need = n_sel - n_above
        cut_is_off = n_kept != jnp.minimum(n_equal, need)

        @pl.when(jnp.max(as_f32(cut_is_off)) > 0.0)
        def _():
            c = jnp.zeros((1, t), i32)
            n_bits = max(1, int(ak_ref.shape[0] - 1).bit_length())
            for b in range(n_bits, -1, -1):
                cand = c | (1 << b)
                (below,) = over_chunks(lambda sc, idx: (as_f32((sc == thr) & (idx < cand)),), (0.0,), (jnp.add,))
                c = jnp.where(below <= need, cand, c)
            cut_ref[...] = c

        attention(thr, cut_ref[...], False)

    att_t = jnp.concatenate(
        [acc_ref[hh, 0:HEAD_DIM, :] / acc_ref[hh, HEAD_DIM:HEAD_DIM + 1, :] for hh in range(ATT_HEADS)], axis=0)
    o_ref[...] = att_t.T.astype(o_ref.dtype)


def _dsa_call(aqp, iqt, gt, ak, ik, avt, n_sel, b, s, t):
    nq = s // t

    def qblk_t(r):
        return pl.BlockSpec((r, t), lambda bi, qi: (0, bi * nq + qi))

    def kblk(w):
        return pl.BlockSpec((None, s, w), lambda bi, qi: (bi, 0, 0))

    return pl.pallas_call(
        functools.partial(_dsa_kernel, n_sel=n_sel, t=t),
        grid=(b, nq),
        in_specs=[qblk_t(ATT_HEADS * Q_PAD), qblk_t(IDX_W), qblk_t(GT_ROWS),
                  pl.BlockSpec((GT_ROWS, s), lambda bi, qi: (0, bi)), kblk(KV_W), kblk(LANES),
                  pl.BlockSpec((nq, ATT_KV_HEADS * V_AUG, t), lambda bi, qi: (bi, 0, 0))],
        out_specs=pl.BlockSpec((None, t, ATT_W), lambda bi, qi: (bi, qi, 0)),
        out_shape=jax.ShapeDtypeStruct((b, s, ATT_W), bf16),
        scratch_shapes=[
            pltpu.VMEM((nq, t, t), f32),
            pltpu.VMEM((KEY_BITS, nq, SUBLANES, t), i32),
            pltpu.VMEM((ATT_HEADS, nq, t, t), f32),
            pltpu.VMEM((1, t), i32),
            pltpu.VMEM((ATT_HEADS, t), f32),
            pltpu.VMEM((ATT_HEADS, V_AUG, t), f32),
            pltpu.VMEM((V_AUG - HEAD_DIM, t), f32),
            pltpu.VMEM((1, t), i32),
            pltpu.VMEM((1, t), f32),
            pltpu.VMEM((nq, SUBLANES, t), i32),
        ],
        compiler_params=pltpu.CompilerParams(dimension_semantics=("parallel", "arbitrary"),
                                             vmem_limit_bytes=VMEM_LIMIT),
        name="dsa",
    )(aqp, iqt, gt, gt, ak, ik, avt)


def _mlstm_kernel(mq_ref, mk_ref, mv_ref, mo_ref, gt_ref, gb_ref, cwq_ref, cwk_ref, cbq_ref, cbk_ref, gmh_ref,
                  o_ref, *, chunk):
    hd = pl.program_id(1)
    seq = mq_ref.shape[0]
    n_chunks = seq // chunk
    row = lax.broadcasted_iota(i32, (chunk, chunk), 0)
    col = lax.broadcasted_iota(i32, (chunk, chunk), 1)
    tril = col <= row
    eye = col == row
    row_t = lax.broadcasted_iota(i32, (SUBLANES, ML_DIM), 0)

    def conv_silu(x_ref, w_ref, b_ref, c):
        t0 = c * chunk
        cur = x_ref[t0:t0 + chunk, :]
        acc = cur * w_ref[CONV_W - 1:CONV_W, :] + b_ref[...]
        for j in range(1, CONV_W):
            if c > 0:
                shifted = x_ref[t0 - j:t0 - j + chunk, :]
            else:
                rolled = pltpu.roll(cur, j, 0)
                first = jnp.where(row_t >= j, rolled[0:SUBLANES], 0.0)
                shifted = jnp.concatenate([first, rolled[SUBLANES:]], axis=0)
            acc = acc + shifted * w_ref[CONV_W - 1 - j:CONV_W - j, :]
        return acc * _sigmoid(acc)

    c_state = jnp.zeros((ML_DIM, ML_DIM), f32)
    n_state = jnp.zeros((1, ML_DIM), f32)
    m_state = jnp.zeros((1, 1), f32)
    li_all = gt_ref[pl.ds(GT_MI + hd, 1), :] + gb_ref[pl.ds(hd, 1), :]
    f_all = gt_ref[pl.ds(GT_MF + hd, 1), :] + gb_ref[pl.ds(ML_HEADS + hd, 1), :]
    lf_all = -(jnp.maximum(-f_all, 0.0) + jnp.log1p(jnp.exp(-jnp.abs(f_all))))
    for c in range(n_chunks):
        t0 = c * chunk
        q = conv_silu(mq_ref, cwq_ref, cbq_ref, c) * (ML_DIM ** -0.5)
        k = conv_silu(mk_ref, cwk_ref, cbk_ref, c)
        qb = q.astype(bf16)
        kb = k.astype(bf16)
        vb = mv_ref[t0:t0 + chunk, :]
        li_row = li_all[:, t0:t0 + chunk]
        lf_row = lf_all[:, t0:t0 + chunk]
        b_col = jnp.sum(jnp.where(tril, lf_row, 0.0), axis=1, keepdims=True)
        b_row = jnp.sum(jnp.where(eye, b_col, 0.0), axis=0, keepdims=True)
        li_col = jnp.sum(jnp.where(eye, li_row, 0.0), axis=1, keepdims=True)
        b_last = jnp.sum(lf_row, axis=1, keepdims=True)

        log_d = jnp.where(tril, b_col - b_row + li_row, -jnp.inf)
        inter = b_col + m_state
        m_t = jnp.maximum(inter, jnp.max(log_d, axis=1, keepdims=True))
        dmat = jnp.exp(log_d - m_t)
        inter_w = jnp.exp(inter - m_t)
        qk = _dot_nt(qb, kb) * dmat
        num = inter_w * _dot(qb, c_state.astype(bf16)) + _dot(qk.astype(bf16), vb)
        den = inter_w * jnp.sum(q * n_state, axis=1, keepdims=True) + jnp.sum(qk, axis=1, keepdims=True)
        h_t = num / jnp.maximum(jnp.abs(den), jnp.exp(-m_t))

        log_g = b_last - b_col + li_col
        m_new = jnp.maximum(b_last + m_state, jnp.max(log_g, axis=0, keepdims=True))
        g = jnp.exp(log_g - m_new)
        decay = jnp.exp(b_last + m_state - m_new)
        gk = g * k
        c_state = decay * c_state + _dot(gk.T.astype(bf16), vb)
        n_state = decay * n_state + jnp.sum(gk, axis=0, keepdims=True)
        m_state = m_new

        hn = h_t * lax.rsqrt(jnp.mean(h_t * h_t, axis=1, keepdims=True) + EPS) * gmh_ref[...]
        o_ref[t0:t0 + chunk, :] = (_sigmoid(mo_ref[t0:t0 + chunk, :]) * hn).astype(o_ref.dtype)


def _mlstm_call(mq, mk, mv, mo, gt, gb, conv_w, conv_b, gmh):
    b, s, _ = mq.shape
    chunk = min(ML_CHUNK, s)

    def head_blk():
        return pl.BlockSpec((None, s, ML_DIM), lambda bi, hi: (bi, 0, hi))

    return pl.pallas_call(
        functools.partial(_mlstm_kernel, chunk=chunk),
        grid=(b, ML_HEADS),
        in_specs=[head_blk(), head_blk(), head_blk(), head_blk(),
                  pl.BlockSpec((GT_ROWS, s), lambda bi, hi: (0, bi)),
                  pl.BlockSpec((2 * ML_HEADS, 1), lambda bi, hi: (0, 0)),
                  pl.BlockSpec((CONV_W, ML_DIM), lambda bi, hi: (0, hi)),
                  pl.BlockSpec((CONV_W, ML_DIM), lambda bi, hi: (0, ML_HEADS + hi)),
                  pl.BlockSpec((1, ML_DIM), lambda bi, hi: (0, hi)),
                  pl.BlockSpec((1, ML_DIM), lambda bi, hi: (0, ML_HEADS + hi)),
                  pl.BlockSpec((None, 1, ML_DIM), lambda bi, hi: (hi, 0, 0))],
        out_specs=head_blk(),
        out_shape=jax.ShapeDtypeStruct((b, s, ML_W), bf16),
        compiler_params=pltpu.CompilerParams(dimension_semantics=("parallel", "parallel"),
                                             vmem_limit_bytes=VMEM_LIMIT),
        name="mlstm",
    )(mq, mk, mv, mo, gt, gb, conv_w, conv_w, conv_b, conv_b, gmh)


def _mix_kernel(h_ref, att_ref, hm_ref, p_ref, wo_ref, gmlp_ref, w1_ref, w2_ref, gple_ref, wg_ref, bg_ref, wp_ref,
                o_ref):
    def rms(v, g_ref):
        return (v * lax.rsqrt(jnp.mean(v * v, axis=-1, keepdims=True) + EPS) * g_ref[...]).astype(bf16)

    mixed = _dot(att_ref[...], wo_ref[0:ATT_W, :]) + _dot(hm_ref[...], wo_ref[ATT_W:ATT_W + ML_W, :])
    h1 = h_ref[...] + mixed
    xn = rms(h1, gmlp_ref)
    mlp = None
    for f in range(D_FF // FF_CHUNK):
        u = jnp.maximum(_dot(xn, w1_ref[:, f * FF_CHUNK:(f + 1) * FF_CHUNK]), 0.0)
        part = _dot((u * u).astype(bf16), w2_ref[f * FF_CHUNK:(f + 1) * FF_CHUNK, :])
        mlp = part if mlp is None else mlp + part
    h2 = h1 + mlp
    gate = _sigmoid(_dot(rms(h2, gple_ref), wg_ref[...]) + bg_ref[...])
    o_ref[...] = h2 + gate * _dot(p_ref[...].astype(bf16), wp_ref[...])


def _mix_call(h, att, hm, p, wo, gmlp, w1, w2, gple, wg, bg, wp):
    n = h.shape[0]
    tm = min(MIX_TM, n)

    def tok(w):
        return pl.BlockSpec((tm, w), lambda i: (i, 0))

    def full(a):
        return pl.BlockSpec(a.shape, lambda i: (0,) * a.ndim, pipeline_mode=pl.Buffered(1))

    return pl.pallas_call(
        _mix_kernel, grid=(n // tm,),
        in_specs=[tok(D_MODEL), tok(ATT_W), tok(ML_W), tok(PLE_DIM)]
                 + [full(a) for a in (wo, gmlp, w1, w2, gple, wg, bg, wp)],
        out_specs=tok(D_MODEL),
        out_shape=jax.ShapeDtypeStruct((n, D_MODEL), f32),
        compiler_params=pltpu.CompilerParams(dimension_semantics=("parallel",), vmem_limit_bytes=VMEM_LIMIT),
        name="mix",
    )(h, att, hm, p, wo, gmlp, w1, w2, gple, wg, bg, wp)


def kernel(x, p, positions, g_mix, w_in, g_cq, w_q_up, w_iq_up, g_qn, g_kn, g_ik, conv_w, conv_b, i_bias, f_bias,
           g_mh, w_out, g_mlp, w_ff1, w_ff2, g_ple, w_ple_gate, b_ple_gate, w_ple):
    b, s, d = x.shape
    n = b * s
    depth = p.shape[0]
    n_sel = min(TOPK_MAX, s // 4)
    t = min(DSA_T, s)
    h = x.reshape(n, d)
    pos = positions.reshape(1, n).astype(i32)
    invf = (ROPE_THETA ** (-(jnp.arange(ROT_HALF, dtype=f32) * 2.0) / ROT_DIM))[:, None]

    split = np.cumsum(IN_SIZES)[:-1].tolist()
    for i in range(depth):
        c_q, a_k, a_v, i_k, i_w, m_q, m_k, m_v, m_o, m_i, m_f = jnp.split(w_in[i], split, axis=1)
        wmain = jnp.concatenate([c_q, m_q, m_k, m_v, m_o], axis=1).astype(bf16)
        wt = jnp.concatenate([a_k, i_k, a_v, i_w, m_i, m_f], axis=1).T.astype(bf16)
        wqupt = jnp.concatenate([w_q_up[i], w_iq_up[i]], axis=1).T.astype(bf16)
        aqt, iqt, ak, ik, avt, gt, mq, mk, mv, mo = _proj_call(
            h, pos, g_mix[i][None, :], wmain, wt, g_cq[i][None, :], wqupt,
            g_qn[i][:, None], g_kn[i][:, None], g_ik[i][:, None], invf, t)

        def b3(a):
            return a.reshape(b, s, a.shape[-1])

        att = _dsa_call(aqt, iqt, gt, b3(ak), b3(ik), avt, n_sel, b, s, t)
        gb = jnp.concatenate([i_bias[i], f_bias[i]])[:, None]
        hm = _mlstm_call(b3(mq), b3(mk), b3(mv), b3(mo), gt, gb, conv_w[i], conv_b[i][None, :],
                         g_mh[i][:, None, :])
        h = _mix_call(h, att.reshape(n, ATT_W), hm.reshape(n, ML_W), p[i].reshape(n, PLE_DIM),
                      w_out[i].astype(bf16), g_mlp[i][None, :], w_ff1[i].astype(bf16), w_ff2[i].astype(bf16),
                      g_ple[i][None, :], w_ple_gate[i].astype(bf16), b_ple_gate[i][None, :], w_ple[i].astype(bf16))
    return h.reshape(b, s, d)
```
